```python
import math
import jax, jax.numpy as jnp
from jax import lax
import numpy as np

D_MODEL = 1024
BATCH = 4
SEQ = 8192
DEPTH = 2
DEC_BATCH = 32
DEC_SEQ = 4
PAST_LEN = 16384
PAGE_SIZE = 128

H_SB = 8
D_SB = 64
W_SB = H_SB * D_SB
D_RNN = 512
N_RG_BLOCKS = 8
RG_BLOCK = D_RNN // N_RG_BLOCKS
RG_C = 8.0
CONV_W = 4
H_DIFF = 4
D_DIFF = 64
W_DIFF = H_DIFF * 2 * D_DIFF
ROT_DIM = D_DIFF // 4
ROPE_THETA = 500000.0
Q_BLOCK = 128
N_BRANCH = 3
D_IN = 4 * W_SB + 2 * D_RNN + 4 * W_DIFF + N_BRANCH * D_MODEL
DN_ALPHA = (2 * DEPTH) ** 0.25
DN_BETA = (8 * DEPTH) ** -0.25
LN_EPS = 1e-5
POOL_NUM = 5
POOL_DEN = 4

kernel_name = 'hybrid_sb_rglru_diffattn_step'


def layer_norm(x, g, b):
    xf = x.astype(jnp.float32)
    mu = jnp.mean(xf, axis=-1, keepdims=True)
    var = jnp.mean(jnp.square(xf - mu), axis=-1, keepdims=True)
    return ((xf - mu) * lax.rsqrt(var + LN_EPS) * g + b).astype(x.dtype)


def rms_norm(x, g):
    xf = x.astype(jnp.float32)
    return (xf * lax.rsqrt(jnp.mean(xf * xf, axis=-1, keepdims=True) + LN_EPS) * g).astype(x.dtype)


def rope_partial(x, pos):
    half = ROT_DIM // 2
    freqs = ROPE_THETA ** (-jnp.arange(half, dtype=jnp.float32) / half)
    ang = pos.astype(jnp.float32)[:, None] * freqs
    cos = jnp.cos(ang)[None, :, None, None, :]
    sin = jnp.sin(ang)[None, :, None, None, :]
    xr = x[..., :ROT_DIM].astype(jnp.float32)
    x1, x2 = xr[..., :half], xr[..., half:]
    rot = jnp.concatenate([x1 * cos - x2 * sin, x2 * cos + x1 * sin], axis=-1).astype(x.dtype)
    return jnp.concatenate([rot, x[..., ROT_DIM:]], axis=-1)


def sb_weights(z, mask):
    z = z[:, :, 0]
    log_1m = jnp.where(mask, jax.nn.log_sigmoid(-z), 0.0)
    later = lax.cumsum(log_1m, axis=3, reverse=True) - log_1m
    return jnp.where(mask, jnp.exp(jax.nn.log_sigmoid(z) + later), 0.0)


def diff_weights(z, mask, lam):
    p = jax.nn.softmax(jnp.where(mask, z, -jnp.inf), axis=-1)
    return p[:, :, 0] - lam * p[:, :, 1]


def mix_attend(q, ks, vs, qpos, kpos, scale, mixer, lam=None):
    z = jnp.concatenate([jnp.einsum('bqhcd,bkhcd->bhcqk', q, k, preferred_element_type=jnp.float32)
                         for k in ks], axis=-1) * scale
    if mixer == 'sb':
        w = sb_weights(z, kpos[None, :] < qpos[:, None])
    else:
        w = diff_weights(z, kpos[None, :] <= qpos[:, None], lam)
    w = w.astype(vs[0].dtype)
    out = None
    start = 0
    for v in vs:
        n = v.shape[1]
        part = jnp.einsum('bhqk,bkhe->bqhe', w[..., start:start + n], v)
        out = part if out is None else out + part
        start += n
    return out


def attend_prompt(q, k, v, scale, mixer, lam=None):
    B, T = q.shape[0], q.shape[1]
    n_blocks = T // Q_BLOCK
    kpos = jnp.arange(T)

    def block(i):
        qi = lax.dynamic_slice_in_dim(q, i * Q_BLOCK, Q_BLOCK, axis=1)
        qpos = i * Q_BLOCK + jnp.arange(Q_BLOCK)
        return mix_attend(qi, [k], [v], qpos, kpos, scale, mixer, lam)

    out = lax.map(block, jnp.arange(n_blocks))
    return jnp.moveaxis(out, 0, 1).reshape(B, T, out.shape[3], out.shape[4])


def causal_conv(x, buf, w, b):
    T = x.shape[1]
    xp = jnp.concatenate([buf.astype(x.dtype), x], axis=1)
    y = b
    for k in range(CONV_W):
        y = y + xp[:, k:k + T] * w[k]
    return y, xp[:, xp.shape[1] - (CONV_W - 1):]


def rg_lru(xc, h0, w_a, b_a, w_x, b_x, lam):
    B, T, _ = xc.shape
    xb = xc.reshape(B, T, N_RG_BLOCKS, RG_BLOCK)
    r = jax.nn.sigmoid(jnp.einsum('btni,nij->btnj', xb, w_a).reshape(B, T, D_RNN) + b_a)
    i = jax.nn.sigmoid(jnp.einsum('btni,nij->btnj', xb, w_x).reshape(B, T, D_RNN) + b_x)
    log_a = -RG_C * jax.nn.softplus(-lam.astype(jnp.float32)) * r.astype(jnp.float32)
    a = jnp.exp(log_a)
    mult = jnp.sqrt(-jnp.expm1(2.0 * log_a))
    bterm = mult * (i * xc).astype(jnp.float32)
    bterm = bterm.at[:, 0].add(a[:, 0] * h0.astype(jnp.float32))

    def combine(e1, e2):
        a1, b1 = e1
        a2, b2 = e2
        return a1 * a2, a2 * b1 + b2

    _, h = lax.associative_scan(combine, (a, bterm), axis=1)
    return h, h[:, -1]


def trunk_layer(x, pos, l, p, past):
    B, T, _ = x.shape
    h = x @ p['w_in'] + p['b_in']
    widths = [W_SB] * 4 + [D_RNN] * 2 + [W_DIFF] * 4 + [N_BRANCH * D_MODEL]
    offs = np.cumsum(widths)[:-1].tolist()
    q_sb, k_sb, v_sb, g_sb, x_rg, g_rg, q_df, k_df, v_df, g_df, br_gate = jnp.split(h, offs, axis=-1)

    q_sb = q_sb.reshape(B, T, H_SB, 1, D_SB)
    k_sb = k_sb.reshape(B, T, H_SB, 1, D_SB)
    v_sb = v_sb.reshape(B, T, H_SB, D_SB)
    sc_sb = D_SB ** -0.5
    q_df = rope_partial(q_df.reshape(B, T, H_DIFF, 2, D_DIFF), pos)
    k_df = rope_partial(k_df.reshape(B, T, H_DIFF, 2, D_DIFF), pos)
    v_df = v_df.reshape(B, T, H_DIFF, 2 * D_DIFF)
    sc_df = D_DIFF ** -0.5
    lam_init = 0.8 - 0.6 * math.exp(-0.3 * l)
    lam = (jnp.exp(jnp.sum(p['lam_q1'].astype(jnp.float32) * p['lam_k1'].astype(jnp.float32)))
           - jnp.exp(jnp.sum(p['lam_q2'].astype(jnp.float32) * p['lam_k2'].astype(jnp.float32))) + lam_init)

    if past is None:
        o_sb = attend_prompt(q_sb, k_sb, v_sb, sc_sb, 'sb')
        o_df = attend_prompt(q_df, k_df, v_df, sc_df, 'diff', lam)
        conv_buf = jnp.zeros((B, CONV_W - 1, D_RNN), x.dtype)
        h0 = jnp.zeros((B, D_RNN), jnp.float32)
    else:
        past_len = past['k_sb'].shape[1]
        kpos = jnp.arange(past_len + T)
        o_sb = mix_attend(q_sb, [past['k_sb'], k_sb], [past['v_sb'], v_sb], pos, kpos, sc_sb, 'sb')
        o_df = mix_attend(q_df, [past['k_df'], k_df], [past['v_df'], v_df], pos, kpos, sc_df, 'diff', lam)
        conv_buf = past['conv']
        h0 = past['h']

    xc, new_buf = causal_conv(x_rg, conv_buf, p['conv_w'], p['conv_b'])
    h_rg, h_last = rg_lru(xc, h0, p['rg_w_a'], p['rg_b_a'], p['rg_w_x'], p['rg_b_x'], p['rg_lambda'])

    o_df = rms_norm(o_df, p['subln_g']) * (1.0 - lam_init)

    y_sb = (o_sb.reshape(B, T, W_SB) * jax.nn.silu(g_sb)) @ p['w_br_sb']
    y_rg = (h_rg.astype(x.dtype) * jax.nn.silu(g_rg)) @ p['w_br_rg']
    y_df = (o_df.reshape(B, T, W_DIFF) * jax.nn.silu(g_df)) @ p['w_br_df']
    gate = jax.nn.sigmoid(br_gate.reshape(B, T, N_BRANCH, D_MODEL))
    merged = gate[:, :, 0] * y_sb + gate[:, :, 1] * y_rg + gate[:, :, 2] * y_df
    x_out = layer_norm(DN_ALPHA * x + merged @ p['w_out'], p['ln_g'], p['ln_b'])

    new = (k_sb.reshape(B, T, H_SB, D_SB), v_sb,
           k_df.reshape(B, T, H_DIFF, 2 * D_DIFF), v_df, new_buf, h_last)
    return x_out, new


def setup_inputs(seed: int = 0) -> dict:
    key = jax.random.key(seed)
    ks = iter(list(jax.random.split(key, 40)))
    f32 = jnp.float32
    n_pages = PAST_LEN // PAGE_SIZE
    n_pool = (DEC_BATCH * n_pages * POOL_NUM) // POOL_DEN

    def nrm(shape, s):
        return s * jax.random.normal(next(ks), shape, f32)

    x_prompt = nrm((BATCH, SEQ, D_MODEL), 1.0)
    x_sample = nrm((DEC_BATCH, DEC_SEQ, D_MODEL), 1.0)
    cache_sb_k = nrm((DEPTH, n_pool, PAGE_SIZE, H_SB, D_SB), 1.0)
    cache_sb_v = nrm((DEPTH, n_pool, PAGE_SIZE, H_SB, D_SB), 1.0)
    cache_diff_k = nrm((DEPTH, n_pool, PAGE_SIZE, H_DIFF, 2 * D_DIFF), 1.0)
    cache_diff_v = nrm((DEPTH, n_pool, PAGE_SIZE, H_DIFF, 2 * D_DIFF), 1.0)
    state_conv = nrm((DEPTH, DEC_BATCH, CONV_W - 1, D_RNN), 1.0)
    state_rglru = nrm((DEPTH, DEC_BATCH, D_RNN), 0.5)
    page_table = jax.random.permutation(next(ks), n_pool)[:DEC_BATCH * n_pages].reshape(
        DEC_BATCH, n_pages).astype(jnp.int32)

    w_in = nrm((DEPTH, D_MODEL, D_IN), D_MODEL ** -0.5)
    b_in = nrm((DEPTH, D_IN), 0.01)
    conv_w = nrm((DEPTH, CONV_W, D_RNN), CONV_W ** -0.5)
    conv_b = nrm((DEPTH, D_RNN), 0.01)
    rg_w_a = nrm((DEPTH, N_RG_BLOCKS, RG_BLOCK, RG_BLOCK), RG_BLOCK ** -0.5)
    rg_b_a = nrm((DEPTH, D_RNN), 0.01)
    rg_w_x = nrm((DEPTH, N_RG_BLOCKS, RG_BLOCK, RG_BLOCK), RG_BLOCK ** -0.5)
    rg_b_x = nrm((DEPTH, D_RNN), 0.01)
    u = jax.random.uniform(next(ks), (DEPTH, D_RNN), f32, 0.9, 0.999)
    a_base = u ** (1.0 / RG_C)
    rg_lambda = jnp.log(a_base) - jnp.log1p(-a_base)
    lam_q1 = nrm((DEPTH, D_DIFF), 0.1)
    lam_k1 = nrm((DEPTH, D_DIFF), 0.1)
    lam_q2 = nrm((DEPTH, D_DIFF), 0.1)
    lam_k2 = nrm((DEPTH, D_DIFF), 0.1)
    subln_g = 1.0 + nrm((DEPTH, 2 * D_DIFF), 0.02)
    w_br_sb = nrm((DEPTH, W_SB, D_MODEL), DN_BETA * W_SB ** -0.5)
    w_br_rg = nrm((DEPTH, D_RNN, D_MODEL), DN_BETA * D_RNN ** -0.5)
    w_br_df = nrm((DEPTH, W_DIFF, D_MODEL), DN_BETA * W_DIFF ** -0.5)
    w_out = nrm((DEPTH, D_MODEL, D_MODEL), DN_BETA * D_MODEL ** -0.5)
    ln_g = 1.0 + nrm((DEPTH, D_MODEL), 0.02)
    ln_b = nrm((DEPTH, D_MODEL), 0.02)
    return {'x_prompt': x_prompt, 'x_sample': x_sample,
            'cache_sb_k': cache_sb_k, 'cache_sb_v': cache_sb_v,
            'cache_diff_k': cache_diff_k, 'cache_diff_v': cache_diff_v,
            'state_conv': state_conv, 'state_rglru': state_rglru, 'page_table': page_table,
            'w_in': w_in, 'b_in': b_in, 'conv_w': conv_w, 'conv_b': conv_b,
            'rg_w_a': rg_w_a, 'rg_b_a': rg_b_a, 'rg_w_x': rg_w_x, 'rg_b_x': rg_b_x,
            'rg_lambda': rg_lambda, 'lam_q1': lam_q1, 'lam_k1': lam_k1,
            'lam_q2': lam_q2, 'lam_k2': lam_k2, 'subln_g': subln_g,
            'w_br_sb': w_br_sb, 'w_br_rg': w_br_rg, 'w_br_df': w_br_df,
            'w_out': w_out, 'ln_g': ln_g, 'ln_b': ln_b}


def reference(x_prompt, x_sample, cache_sb_k, cache_sb_v, cache_diff_k, cache_diff_v,
              state_conv, state_rglru, page_table, w_in, b_in, conv_w, conv_b,
              rg_w_a, rg_b_a, rg_w_x, rg_b_x, rg_lambda, lam_q1, lam_k1, lam_q2, lam_k2,
              subln_g, w_br_sb, w_br_rg, w_br_df, w_out, ln_g, ln_b):
    n_dec, n_pages = page_table.shape
    past_len = n_pages * cache_sb_k.shape[2]
    pos_p = jnp.arange(x_prompt.shape[1])
    pos_s = past_len + jnp.arange(x_sample.shape[1])
    new_p = []
    new_s = []
    for l in range(DEPTH):
        p = {'w_in': w_in[l], 'b_in': b_in[l], 'conv_w': conv_w[l], 'conv_b': conv_b[l],
             'rg_w_a': rg_w_a[l], 'rg_b_a': rg_b_a[l], 'rg_w_x': rg_w_x[l], 'rg_b_x': rg_b_x[l],
             'rg_lambda': rg_lambda[l], 'lam_q1': lam_q1[l], 'lam_k1': lam_k1[l],
             'lam_q2': lam_q2[l], 'lam_k2': lam_k2[l], 'subln_g': subln_g[l],
             'w_br_sb': w_br_sb[l], 'w_br_rg': w_br_rg[l], 'w_br_df': w_br_df[l],
             'w_out': w_out[l], 'ln_g': ln_g[l], 'ln_b': ln_b[l]}
        past = {'k_sb': cache_sb_k[l, page_table].reshape(n_dec, past_len, H_SB, 1, D_SB),
                'v_sb': cache_sb_v[l, page_table].reshape(n_dec, past_len, H_SB, D_SB),
                'k_df': cache_diff_k[l, page_table].reshape(n_dec, past_len, H_DIFF, 2, D_DIFF),
                'v_df': cache_diff_v[l, page_table].reshape(n_dec, past_len, H_DIFF, 2 * D_DIFF),
                'conv': state_conv[l], 'h': state_rglru[l]}
        x_prompt, st_p = trunk_layer(x_prompt, pos_p, l, p, None)
        x_sample, st_s = trunk_layer(x_sample, pos_s, l, p, past)
        new_p.append(st_p)
        new_s.append(st_s)
    p_sb_k = jnp.stack([r[0] for r in new_p])
    p_sb_v = jnp.stack([r[1] for r in new_p])
    p_diff_k = jnp.stack([r[2] for r in new_p])
    p_diff_v = jnp.stack([r[3] for r in new_p])
    p_conv = jnp.stack([r[4] for r in new_p])
    p_h = jnp.stack([r[5] for r in new_p])
    s_sb_k = jnp.stack([r[0] for r in new_s])
    s_sb_v = jnp.stack([r[1] for r in new_s])
    s_diff_k = jnp.stack([r[2] for r in new_s])
    s_diff_v = jnp.stack([r[3] for r in new_s])
    s_conv = jnp.stack([r[4] for r in new_s])
    s_h = jnp.stack([r[5] for r in new_s])
    return (x_prompt, x_sample, p_sb_k, p_sb_v, p_diff_k, p_diff_v, p_conv, p_h,
            s_sb_k, s_sb_v, s_diff_k, s_diff_v, s_conv, s_h)
```

```python
import functools
import math

import jax
import jax.numpy as jnp
from jax import lax
from jax.experimental import pallas as pl
from jax.experimental.pallas import tpu as pltpu

F32 = jnp.float32
BF16 = jnp.bfloat16

D_HEAD = 64
H_SB = 8
H_DIFF = 4
W_ATT = 512
D_RNN = 512
N_RG_BLOCKS = 8
RG_C = 8.0
CONV_W = 4
ROT_DIM = D_HEAD // 4
ROPE_THETA = 500000.0
N_BRANCH = 3
LN_EPS = 1e-5
QK_SCALE = D_HEAD ** -0.5

LANES = 128
SUBLANES = 8
VMEM_LIMIT_BYTES = 56 * 1024 * 1024

NEG_BIG = -1e30


def _softplus(z):
    return jnp.maximum(z, 0.0) + jnp.log1p(jnp.exp(-jnp.abs(z)))


def _sigmoid(z):
    return 1.0 / (1.0 + jnp.exp(-z))


def _split_bf16(x):
    hi = x.astype(BF16)
    lo = (x - hi.astype(F32)).astype(BF16)
    return hi, lo


def _params(*sem):
    return pltpu.CompilerParams(dimension_semantics=sem, vmem_limit_bytes=VMEM_LIMIT_BYTES)


def _rope_table_kernel(freq_ref, sgn_ref, cos_ref, sin_ref, *, pos0, period):
    rows = cos_ref.shape[0]
    r = lax.broadcasted_iota(jnp.int32, (rows, LANES), 0)
    pos = (pos0 + r % period).astype(F32)
    ang = pos * freq_ref[...]
    cos_ref[...] = jnp.cos(ang)
    sin_ref[...] = jnp.sin(ang) * sgn_ref[...]


def _rope_tables(rows, pos0, period):
    half = ROT_DIM // 2
    freqs = ROPE_THETA ** (-jnp.arange(half, dtype=F32) / half)
    d = jnp.arange(LANES) % D_HEAD
    freq_lane = jnp.where(d < ROT_DIM, freqs[d % half], 0.0).astype(F32)[None, :]
    sgn_lane = jnp.where(d < half, -1.0, jnp.where(d < ROT_DIM, 1.0, 0.0)).astype(F32)[None, :]
    return pl.pallas_call(
        functools.partial(_rope_table_kernel, pos0=pos0, period=period),
        out_shape=(jax.ShapeDtypeStruct((rows, LANES), F32),) * 2,
        name="rope_table",
    )(freq_lane, sgn_lane)


_COL_Q_SB, _COL_K_SB, _COL_V_SB, _COL_G_SB = 0, 1, 2, 3
_COL_X_RG, _COL_G_RG = 4, 5
_COL_Q_DF, _COL_K_DF, _COL_V_DF, _COL_G_DF = 6, 7, 8, 9
_COL_GATE = 10


def _in_proj_kernel(x_ref, w_ref, b_ref, cos_ref, sin_ref,
                    q_sb, k_sb, v_sb, g_sb, x_rg, g_rg, q_df, k_df, v_df, g_df, gate):
    xb = x_ref[...].astype(BF16)

    def proj(col, sub=0, width=W_ATT):
        off = col * W_ATT + sub * width
        return (jnp.dot(xb, w_ref[:, off:off + width], preferred_element_type=F32)
                + b_ref[:, off:off + width])

    cos = cos_ref[...]
    sin = sin_ref[...]
    d = lax.broadcasted_iota(jnp.int32, cos.shape, 1) % D_HEAD
    first_half = d < (ROT_DIM // 2)

    def rope(h, out_ref, scale):
        for j in range(W_ATT // LANES):
            xj = h[:, j * LANES:(j + 1) * LANES]
            up = pltpu.roll(xj, LANES - ROT_DIM // 2, 1)
            dn = pltpu.roll(xj, ROT_DIM // 2, 1)
            r = xj * cos + jnp.where(first_half, up, dn) * sin
            if scale != 1.0:
                r = r * scale
            out_ref[:, j * LANES:(j + 1) * LANES] = r.astype(out_ref.dtype)

    q_sb[...] = (proj(_COL_Q_SB) * QK_SCALE).astype(q_sb.dtype)
    k_sb[...] = proj(_COL_K_SB)
    v_sb[...] = proj(_COL_V_SB)
    g_sb[...] = proj(_COL_G_SB)
    x_rg[...] = proj(_COL_X_RG)
    g_rg[...] = proj(_COL_G_RG)
    rope(proj(_COL_Q_DF), q_df, QK_SCALE)
    rope(proj(_COL_K_DF), k_df, 1.0)
    v_df[...] = proj(_COL_V_DF)
    g_df[...] = proj(_COL_G_DF)
    d_model = gate.shape[1] // N_BRANCH
    for j in range(N_BRANCH):
        gate[:, j * d_model:(j + 1) * d_model] = proj(_COL_GATE, j, d_model)


def _in_proj(x2d, w_bf, b, cos_t, sin_t, q_dtype, tm):
    m, d_model = x2d.shape
    d_in = w_bf.shape[1]
    t_rows = cos_t.shape[0]
    n_t = t_rows // tm
    row = lambda i: (i, 0)
    const = lambda i: (0, 0)
    tab = lambda i: (i % n_t, 0)
    att = lambda dt: jax.ShapeDtypeStruct((m, W_ATT), dt)
    out_shape = (att(q_dtype), att(F32), att(F32), att(F32), att(F32), att(F32),
                 att(q_dtype), att(F32), att(F32), att(F32),
                 jax.ShapeDtypeStruct((m, N_BRANCH * d_model), F32))
    out_specs = tuple(pl.BlockSpec((tm, s.shape[1]), row) for s in out_shape)
    return pl.pallas_call(
        _in_proj_kernel,
        grid=(m // tm,),
        in_specs=[pl.BlockSpec((tm, d_model), row),
                  pl.BlockSpec((d_model, d_in), const, pipeline_mode=pl.Buffered(1)),
                  pl.BlockSpec((1, d_in), const, pipeline_mode=pl.Buffered(1)),
                  pl.BlockSpec((tm, LANES), tab),
                  pl.BlockSpec((tm, LANES), tab)],
        out_specs=out_specs,
        out_shape=out_shape,
        compiler_params=_params("parallel"),
        name="in_proj",
    )(x2d, w_bf, b, cos_t, sin_t)


_NT = (((1,), (1,)), ((), ()))
_TN = (((0,), (0,)), ((), ()))


def _sb_prompt_kernel(q_ref, k_ref, v_ref, o_ref, acc_ref, *, tq):
    i = pl.program_id(2)
    q = q_ref[...]
    lane = lax.broadcasted_iota(jnp.int32, q.shape, 1)
    zero = jnp.zeros_like(q)
    q_heads = (jnp.where(lane < D_HEAD, q, zero), jnp.where(lane >= D_HEAD, q, zero))
    s_idx = lax.broadcasted_iota(jnp.int32, (tq, tq), 0)
    t_idx = lax.broadcasted_iota(jnp.int32, (tq, tq), 1)
    later_keys = (t_idx > s_idx).astype(BF16)
    diag_valid = s_idx < t_idx

    acc_ref[...] = jnp.zeros_like(acc_ref)

    def block(kb, carries, masked):
        start = pl.multiple_of(kb * tq, tq)
        k = k_ref[pl.ds(start, tq), :].astype(BF16)
        v = v_ref[pl.ds(start, tq), :].astype(BF16)
        new = []
        for h in range(2):
            z = lax.dot_general(k, q_heads[h], _NT, preferred_element_type=F32)
            log_1m = -_softplus(z)
            if masked:
                log_1m = jnp.where(diag_valid, log_1m, 0.0)
            hi, lo = _split_bf16(log_1m)
            later = (jnp.dot(later_keys, hi, preferred_element_type=F32)
                     + jnp.dot(later_keys, lo, preferred_element_type=F32))
            w = jnp.exp(z + log_1m + later + carries[h])
            if masked:
                w = jnp.where(diag_valid, w, 0.0)
            acc_ref[h] += lax.dot_general(v, w.astype(BF16), _TN, preferred_element_type=F32)
            new.append(carries[h] + later[0:1, :] + log_1m[0:1, :])
        return tuple(new)

    carry0 = jnp.zeros((1, tq), F32)
    carries = block(i, (carry0, carry0), True)

    def body(j, carries):
        return block(i - 1 - j, carries, False)

    lax.fori_loop(0, i, body, carries)

    o_t = jnp.concatenate([acc_ref[0, 0:D_HEAD, :], acc_ref[1, D_HEAD:2 * D_HEAD, :]], axis=0)
    o_ref[...] = o_t.T


def _sb_prompt(q, k, v, batch, seq, tq):
    m = q.shape[0]
    nq = seq // tq
    hp = H_SB // 2
    return pl.pallas_call(
        functools.partial(_sb_prompt_kernel, tq=tq),
        grid=(batch, hp, nq),
        in_specs=[pl.BlockSpec((tq, LANES), lambda b, h, i: (b * nq + i, h)),
                  pl.BlockSpec((seq, LANES), lambda b, h, i: (b, h)),
                  pl.BlockSpec((seq, LANES), lambda b, h, i: (b, h))],
        out_specs=pl.BlockSpec((tq, LANES), lambda b, h, i: (b * nq + i, h)),
        out_shape=jax.ShapeDtypeStruct((m, W_ATT), F32),
        scratch_shapes=[pltpu.VMEM((2, LANES, tq), F32)],
        compiler_params=_params("parallel", "parallel", "parallel"),
        name="sb_prompt",
    )(q, k, v)


def _lambda(lam_ref, lam_init):
    p = lam_ref[...]
    s1 = jnp.sum(p[0:1, :] * p[1:2, :], axis=-1, keepdims=True)
    s2 = jnp.sum(p[2:3, :] * p[3:4, :], axis=-1, keepdims=True)
    return jnp.exp(s1) - jnp.exp(s2) + lam_init


def _diff_prompt_kernel(q_ref, k_ref, v_ref, lam_ref, g_ref, o_ref, acc_ref, *, tq, lam_init):
    i = pl.program_id(2)
    q = q_ref[...]
    lane = lax.broadcasted_iota(jnp.int32, q.shape, 1)
    zero = jnp.zeros_like(q)
    q_comp = (jnp.where(lane < D_HEAD, q, zero), jnp.where(lane >= D_HEAD, q, zero))
    s_idx = lax.broadcasted_iota(jnp.int32, (tq, tq), 0)
    t_idx = lax.broadcasted_iota(jnp.int32, (tq, tq), 1)
    diag_valid = s_idx <= t_idx

    acc_ref[...] = jnp.zeros_like(acc_ref)

    def block(kb, state, masked):
        start = pl.multiple_of(kb * tq, tq)
        k = k_ref[pl.ds(start, tq), :].astype(BF16)
        v = v_ref[pl.ds(start, tq), :].astype(BF16)
        new = []
        for c in range(2):
            m_old, l_old = state[c]
            z = lax.dot_general(k, q_comp[c], _NT, preferred_element_type=F32)
            if masked:
                z = jnp.where(diag_valid, z, NEG_BIG)
            m_new = jnp.maximum(m_old, jnp.max(z, axis=0, keepdims=True))
            alpha = jnp.exp(m_old - m_new)
            p = jnp.exp(z - m_new)
            l_new = alpha * l_old + jnp.sum(p, axis=0, keepdims=True)
            pv = lax.dot_general(v, p.astype(BF16), _TN, preferred_element_type=F32)
            acc_ref[c] = alpha * acc_ref[c] + pv
            new.append((m_new, l_new))
        return tuple(new)

    init = (jnp.full((1, tq), NEG_BIG, F32), jnp.zeros((1, tq), F32))
    state = block(i, (init, init), True)

    def body(j, state):
        return block(i - 1 - j, state, False)

    state = lax.fori_loop(0, i, body, state)

    lam = _lambda(lam_ref, lam_init)
    o_t = acc_ref[0] / state[0][1] - lam * (acc_ref[1] / state[1][1])
    ms = jnp.mean(o_t * o_t, axis=0, keepdims=True)
    o_t = o_t * lax.rsqrt(ms + LN_EPS) * g_ref[...] * (1.0 - lam_init)
    o_ref[...] = o_t.T


def _diff_prompt(q, k, v, lam_p, g_col, lam_init, batch, seq, tq):
    m = q.shape[0]
    nq = seq // tq
    return pl.pallas_call(
        functools.partial(_diff_prompt_kernel, tq=tq, lam_init=lam_init),
        grid=(batch, H_DIFF, nq),
        in_specs=[pl.BlockSpec((tq, LANES), lambda b, h, i: (b * nq + i, h)),
                  pl.BlockSpec((seq, LANES), lambda b, h, i: (b, h)),
                  pl.BlockSpec((seq, LANES), lambda b, h, i: (b, h)),
                  pl.BlockSpec((4, D_HEAD), lambda b, h, i: (0, 0)),
                  pl.BlockSpec((2 * D_HEAD, 1), lambda b, h, i: (0, 0))],
        out_specs=pl.BlockSpec((tq, LANES), lambda b, h, i: (b * nq + i, h)),
        out_shape=jax.ShapeDtypeStruct((m, W_ATT), F32),
        scratch_shapes=[pltpu.VMEM((2, LANES, tq), F32)],
        compiler_params=_params("parallel", "parallel", "parallel"),
        name="diff_prompt",
    )(q, k, v, lam_p, g_col)


def _rg_gates(xc, wa_ref, ba_ref, wx_ref, bx_ref, lam_ref):
    xb = xc.astype(BF16)
    r = _sigmoid(jnp.dot(xb, wa_ref[...], preferred_element_type=F32) + ba_ref[...])
    g = _sigmoid(jnp.dot(xb, wx_ref[...], preferred_element_type=F32) + bx_ref[...])
    log_a = (-RG_C * _softplus(-lam_ref[...])) * r
    a = jnp.exp(log_a)
    t = jnp.tanh(log_a)
    mult = jnp.sqrt(-2.0 * t / (1.0 - t))
    return a, mult * (g * xc)


def _rglru_prompt_kernel(x_ref, buf0_ref, h0_ref, cw_ref, cb_ref, wa_ref, ba_ref, wx_ref, bx_ref,
                         lam_ref, h_ref, hlast_ref, xbuf, hstate, *, tc):
    c = pl.program_id(1)
    pad = SUBLANES

    @pl.when(c == 0)
    def _():
        xbuf[0:pad, :] = jnp.zeros((pad, D_RNN), F32)
        xbuf[pad - (CONV_W - 1):pad, :] = buf0_ref[...]
        hstate[...] = h0_ref[...]

    @pl.when(c != 0)
    def _():
        xbuf[0:pad, :] = xbuf[tc:tc + pad, :]

    xbuf[pad:pad + tc, :] = x_ref[...]
    xc = cb_ref[...]
    for kk in range(CONV_W):
        off = pad - (CONV_W - 1) + kk
        xc = xc + xbuf[off:off + tc, :] * cw_ref[kk:kk + 1, :]

    a, b = _rg_gates(xc, wa_ref, ba_ref, wx_ref, bx_ref, lam_ref)

    row = lax.broadcasted_iota(jnp.int32, a.shape, 0)
    dist = 1
    while dist < tc:
        keep = row >= dist
        a_sh = pltpu.roll(a, dist, 0)
        b_sh = pltpu.roll(b, dist, 0)
        b = jnp.where(keep, a * b_sh + b, b)
        a = jnp.where(keep, a * a_sh, a)
        dist *= 2
    h = b + a * hstate[...]
    h_ref[...] = h
    hstate[...] = h[tc - 1:tc, :]

    @pl.when(c == pl.num_programs(1) - 1)
    def _():
        hlast_ref[...] = h[tc - 1:tc, :]


def _rglru_prompt(x_rg, buf0, h0, rgp, batch, seq, tc):
    m = x_rg.shape[0]
    nc = seq // tc
    const = lambda b, c: (0, 0)
    wspec = lambda shape: pl.BlockSpec(shape, const)
    return pl.pallas_call(
        functools.partial(_rglru_prompt_kernel, tc=tc),
        grid=(batch, nc),
        in_specs=[pl.BlockSpec((tc, D_RNN), lambda b, c: (b * nc + c, 0)),
                  pl.BlockSpec((None, CONV_W - 1, D_RNN), lambda b, c: (b, 0, 0)),
                  pl.BlockSpec((None, 1, D_RNN), lambda b, c: (b, 0, 0)),
                  wspec((CONV_W, D_RNN)), wspec((1, D_RNN)),
                  wspec((D_RNN, D_RNN)), wspec((1, D_RNN)),
                  wspec((D_RNN, D_RNN)), wspec((1, D_RNN)), wspec((1, D_RNN))],
        out_specs=(pl.BlockSpec((tc, D_RNN), lambda b, c: (b * nc + c, 0)),
                   pl.BlockSpec((None, 1, D_RNN), lambda b, c: (b, 0, 0))),
        out_shape=(jax.ShapeDtypeStruct((m, D_RNN), F32),
                   jax.ShapeDtypeStruct((batch, 1, D_RNN), F32)),
        scratch_shapes=[pltpu.VMEM((tc + SUBLANES, D_RNN), F32), pltpu.VMEM((1, D_RNN), F32)],
        compiler_params=_params("parallel", "arbitrary"),
        name="rglru_prompt",
    )(x_rg, buf0, h0, *rgp)


def _rglru_decode_kernel(x_ref, buf_ref, h0_ref, cw_ref, cb_ref, wa_ref, ba_ref, wx_ref, bx_ref,
                         lam_ref, h_ref, hlast_ref):
    steps = x_ref.shape[0]
    rows = [buf_ref[j] for j in range(CONV_W - 1)] + [x_ref[t] for t in range(steps)]
    h = h0_ref[...]
    for t in range(steps):
        xc = cb_ref[...]
        for kk in range(CONV_W):
            xc = xc + rows[t + kk] * cw_ref[kk:kk + 1, :]
        a, b = _rg_gates(xc, wa_ref, ba_ref, wx_ref, bx_ref, lam_ref)
        h = a * h + b
        h_ref[t] = h
    hlast_ref[...] = h


def _rglru_decode(x_tm, buf_tm, h0, rgp):
    steps, nb, _ = x_tm.shape
    return pl.pallas_call(
        _rglru_decode_kernel,
        out_shape=(jax.ShapeDtypeStruct((steps, nb, D_RNN), F32),
                   jax.ShapeDtypeStruct((nb, D_RNN), F32)),
        compiler_params=pltpu.CompilerParams(vmem_limit_bytes=VMEM_LIMIT_BYTES),
        name="rglru_decode",
    )(x_tm, buf_tm, h0, *rgp)


def _merge_kernel(o_sb, g_sb, h_rg, g_rg, o_df, g_df, gate, x_ref,
                  w_sb, w_rg, w_df, w_out, lng, lnb, y_ref, *, alpha):
    def branch(o_ref, g_ref, w_ref):
        g = g_ref[...]
        act = o_ref[...] * (g * _sigmoid(g))
        return jnp.dot(act.astype(BF16), w_ref[...], preferred_element_type=F32)

    d_model = x_ref.shape[1]
    merged = None
    for j, (o_r, g_r, w_r) in enumerate(((o_sb, g_sb, w_sb), (h_rg, g_rg, w_rg), (o_df, g_df, w_df))):
        term = _sigmoid(gate[:, j * d_model:(j + 1) * d_model]) * branch(o_r, g_r, w_r)
        merged = term if merged is None else merged + term
    z = alpha * x_ref[...] + jnp.dot(merged.astype(BF16), w_out[...], preferred_element_type=F32)
    mu = jnp.mean(z, axis=-1, keepdims=True)
    zc = z - mu
    var = jnp.mean(zc * zc, axis=-1, keepdims=True)
    y_ref[...] = zc * lax.rsqrt(var + LN_EPS) * lng[...] + lnb[...]


def _merge(o_sb, g_sb, h_rg, g_rg, o_df, g_df, gate, x2d, mp, alpha, tm):
    m, d_model = x2d.shape
    row = lambda i: (i, 0)
    const = lambda i: (0, 0)
    act = pl.BlockSpec((tm, W_ATT), row)
    return pl.pallas_call(
        functools.partial(_merge_kernel, alpha=alpha),
        grid=(m // tm,),
        in_specs=[act] * 6 + [pl.BlockSpec((tm, N_BRANCH * d_model), row),
                              pl.BlockSpec((tm, d_model), row)]
                 + [pl.BlockSpec((W_ATT, d_model), const)] * 3
                 + [pl.BlockSpec((d_model, d_model), const),
                    pl.BlockSpec((1, d_model), const), pl.BlockSpec((1, d_model), const)],
        out_specs=pl.BlockSpec((tm, d_model), row),
        out_shape=jax.ShapeDtypeStruct((m, d_model), F32),
        compiler_params=_params("parallel"),
        name="merge",
    )(o_sb, g_sb, h_rg, g_rg, o_df, g_df, gate, x2d, *mp)


PAGES_PER_STEP = 8
KEY_BLOCK = 256
N_GROUPS = W_ATT // D_HEAD


def _expand_queries(q):
    steps = q.shape[0]
    rep = jnp.concatenate([jnp.broadcast_to(q[t:t + 1, :], (N_GROUPS, W_ATT)) for t in range(steps)],
                          axis=0)
    row = lax.broadcasted_iota(jnp.int32, rep.shape, 0)
    lane = lax.broadcasted_iota(jnp.int32, rep.shape, 1)
    return jnp.where(lane // D_HEAD == row % N_GROUPS, rep, 0.0)


def _page_blocks(k_pages, v_pages):
    pages_per_block = KEY_BLOCK // k_pages[0].shape[0]
    n_blocks = len(k_pages) // pages_per_block
    for blk in reversed(range(n_blocks)):
        sl = slice(blk * pages_per_block, (blk + 1) * pages_per_block)
        k = jnp.concatenate([r[...] for r in k_pages[sl]], axis=0).astype(BF16)
        v = jnp.concatenate([r[...] for r in v_pages[sl]], axis=0).astype(BF16)
        yield k, v


def _sb_decode_kernel(pt_ref, q_ref, kn_ref, vn_ref, *rest):
    del pt_ref
    k_pages = rest[:PAGES_PER_STEP]
    v_pages = rest[PAGES_PER_STEP:2 * PAGES_PER_STEP]
    o_ref, acc_ref, carry_ref = rest[2 * PAGES_PER_STEP:]
    j = pl.program_id(1)
    steps = q_ref.shape[0]
    qx = _expand_queries(q_ref[...])
    rows = qx.shape[0]
    tok = lax.broadcasted_iota(jnp.int32, (rows, 1), 0) // N_GROUPS

    @pl.when(j == 0)
    def _():
        z, log_1m, valid = [], [], []
        for jn in range(steps):
            zj = jnp.sum(qx * kn_ref[jn:jn + 1, :], axis=-1, keepdims=True)
            vj = tok > jn
            z.append(zj)
            valid.append(vj)
            log_1m.append(jnp.where(vj, -_softplus(zj), 0.0))
        later = jnp.zeros((rows, 1), F32)
        acc = jnp.zeros((rows, W_ATT), F32)
        for jn in reversed(range(steps)):
            w = jnp.where(valid[jn], jnp.exp(z[jn] + log_1m[jn] + later), 0.0)
            acc = acc + w * vn_ref[jn:jn + 1, :]
            later = later + log_1m[jn]
        acc_ref[...] = acc
        carry_ref[...] = later

    qb = qx.astype(BF16)
    s_idx = lax.broadcasted_iota(jnp.int32, (KEY_BLOCK, KEY_BLOCK), 0)
    t_idx = lax.broadcasted_iota(jnp.int32, (KEY_BLOCK, KEY_BLOCK), 1)
    later_keys = (s_idx > t_idx).astype(BF16)
    carry = carry_ref[...]
    acc = acc_ref[...]
    for k, v in _page_blocks(k_pages, v_pages):
        z = lax.dot_general(qb, k, _NT, preferred_element_type=F32)
        log_1m = -_softplus(z)
        hi, lo = _split_bf16(log_1m)
        later = (jnp.dot(hi, later_keys, preferred_element_type=F32)
                 + jnp.dot(lo, later_keys, preferred_element_type=F32))
        w = jnp.exp(z + log_1m + later + carry)
        acc = acc + jnp.dot(w.astype(BF16), v, preferred_element_type=F32)
        carry = carry + jnp.sum(log_1m, axis=-1, keepdims=True)
    acc_ref[...] = acc
    carry_ref[...] = carry

    @pl.when(j == pl.num_programs(1) - 1)
    def _():
        row = lax.broadcasted_iota(jnp.int32, acc.shape, 0)
        lane = lax.broadcasted_iota(jnp.int32, acc.shape, 1)
        own = jnp.where(lane // D_HEAD == row % N_GROUPS, acc, 0.0)
        o_ref[...] = jnp.sum(own.reshape(steps, N_GROUPS, W_ATT), axis=1)


def _diff_decode_kernel(pt_ref, q_ref, kn_ref, vn_ref, lam_ref, g_ref, *rest, lam_init):
    del pt_ref
    k_pages = rest[:PAGES_PER_STEP]
    v_pages = rest[PAGES_PER_STEP:2 * PAGES_PER_STEP]
    o_ref, acc_ref, m_ref, l_ref = rest[2 * PAGES_PER_STEP:]
    j = pl.program_id(1)
    steps = q_ref.shape[0]
    qx = _expand_queries(q_ref[...])
    rows = qx.shape[0]
    tok = lax.broadcasted_iota(jnp.int32, (rows, 1), 0) // N_GROUPS

    @pl.when(j == 0)
    def _():
        z = []
        m = jnp.full((rows, 1), NEG_BIG, F32)
        for jn in range(steps):
            zj = jnp.sum(qx * kn_ref[jn:jn + 1, :], axis=-1, keepdims=True)
            zj = jnp.where(tok >= jn, zj, NEG_BIG)
            z.append(zj)
            m = jnp.maximum(m, zj)
        l = jnp.zeros((rows, 1), F32)
        acc = jnp.zeros((rows, W_ATT), F32)
        for jn in range(steps):
            p = jnp.exp(z[jn] - m)
            l = l + p
            acc = acc + p * vn_ref[jn:jn + 1, :]
        acc_ref[...] = acc
        m_ref[...] = m
        l_ref[...] = l

    qb = qx.astype(BF16)
    m = m_ref[...]
    l = l_ref[...]
    acc = acc_ref[...]
    for k, v in _page_blocks(k_pages, v_pages):
        z = lax.dot_general(qb, k, _NT, preferred_element_type=F32)
        m_new = jnp.maximum(m, jnp.max(z, axis=-1, keepdims=True))
        alpha = jnp.exp(m - m_new)
        p = jnp.exp(z - m_new)
        l = alpha * l + jnp.sum(p, axis=-1, keepdims=True)
        acc = alpha * acc + jnp.dot(p.astype(BF16), v, preferred_element_type=F32)
        m = m_new
    acc_ref[...] = acc
    m_ref[...] = m
    l_ref[...] = l

    @pl.when(j == pl.num_programs(1) - 1)
    def _():
        lam = _lambda(lam_ref, lam_init)
        row = lax.broadcasted_iota(jnp.int32, acc.shape, 0)
        lane = lax.broadcasted_iota(jnp.int32, acc.shape, 1)
        comp = lax.broadcasted_iota(jnp.int32, (rows, 1), 0) % 2
        weight = jnp.where(comp == 0, 1.0, -lam) / l
        own = jnp.where(lane // (2 * D_HEAD) == (row % N_GROUPS) // 2, acc * weight, 0.0)
        o = jnp.sum(own.reshape(steps, N_GROUPS, W_ATT), axis=1)
        for h in range(H_DIFF):
            oh = o[:, h * LANES:(h + 1) * LANES]
            ms = jnp.mean(oh * oh, axis=-1, keepdims=True)
            o_ref[:, h * LANES:(h + 1) * LANES] = (
                oh * lax.rsqrt(ms + LN_EPS) * g_ref[...] * (1.0 - lam_init))


def _decode_attention(kernel_fn, layer, page_table, q3, kn3, vn3, cache_k, cache_v, extra, extra_specs,
                      scratch, name, newest_first):
    nb, steps, _ = q3.shape
    n_pages = page_table.shape[1]
    n_chunks = n_pages // PAGES_PER_STEP
    page_rows = cache_k.shape[2]

    def page_spec(p):
        def index(b, j, pt):
            chunk = (n_chunks - 1 - j) if newest_first else j
            return (layer, pt[b, chunk * PAGES_PER_STEP + p], 0, 0)
        return pl.BlockSpec((None, None, page_rows, W_ATT), index)

    tok_spec = pl.BlockSpec((None, steps, W_ATT), lambda b, j, pt: (b, 0, 0))
    in_specs = ([tok_spec] * 3 + extra_specs
                + [page_spec(p) for p in range(PAGES_PER_STEP)] * 2)
    grid_spec = pltpu.PrefetchScalarGridSpec(
        num_scalar_prefetch=1,
        grid=(nb, n_chunks),
        in_specs=in_specs,
        out_specs=tok_spec,
        scratch_shapes=scratch,
    )
    return pl.pallas_call(
        kernel_fn,
        grid_spec=grid_spec,
        out_shape=jax.ShapeDtypeStruct((nb, steps, W_ATT), F32),
        compiler_params=_params("parallel", "arbitrary"),
        name=name,
    )(page_table, q3, kn3, vn3, *extra, *([cache_k] * PAGES_PER_STEP), *([cache_v] * PAGES_PER_STEP))


def _block_diag(w):
    n, bi, bj = w.shape
    eye = jnp.eye(n, dtype=w.dtype)
    return (eye[:, None, :, None] * w[:, :, None, :]).reshape(n * bi, n * bj)


def _row_tile(m, want):
    return want if m % want == 0 else m


def kernel(x_prompt, x_sample, cache_sb_k, cache_sb_v, cache_diff_k, cache_diff_v, state_conv,
           state_rglru, page_table, w_in, b_in, conv_w, conv_b, rg_w_a, rg_b_a, rg_w_x, rg_b_x,
           rg_lambda, lam_q1, lam_k1, lam_q2, lam_k2, subln_g, w_br_sb, w_br_rg, w_br_df, w_out,
           ln_g, ln_b):
    depth = w_in.shape[0]
    batch, seq, d_model = x_prompt.shape
    nb, steps, _ = x_sample.shape
    n_pool, page_rows = cache_sb_k.shape[1], cache_sb_k.shape[2]
    past_len = page_table.shape[1] * page_rows
    alpha = (2 * depth) ** 0.25

    assert cache_sb_k.shape[3] * cache_sb_k.shape[4] == W_ATT
    assert page_table.shape[1] % PAGES_PER_STEP == 0 and KEY_BLOCK % page_rows == 0

    tm_p = _row_tile(batch * seq, 256)
    tq = _row_tile(seq, 256)
    tc = _row_tile(seq, 256)
    m_s = nb * steps

    cos_p, sin_p = _rope_tables(seq, 0, seq)
    cos_s, sin_s = _rope_tables(m_s, past_len, steps)

    cache_shape = (depth, n_pool, page_rows, W_ATT)
    c_sb_k = cache_sb_k.reshape(cache_shape)
    c_sb_v = cache_sb_v.reshape(cache_shape)
    c_df_k = cache_diff_k.reshape(cache_shape)
    c_df_v = cache_diff_v.reshape(cache_shape)

    xp = x_prompt.reshape(batch * seq, d_model)
    xs = x_sample.reshape(m_s, d_model)
    zeros_buf = jnp.zeros((batch, CONV_W - 1, D_RNN), F32)
    zeros_h = jnp.zeros((batch, 1, D_RNN), F32)

    new_p, new_s = [], []
    for l in range(depth):
        lam_init = 0.8 - 0.6 * math.exp(-0.3 * l)
        w_bf = w_in[l].astype(BF16)
        b_l = b_in[l][None, :]
        rgp = (conv_w[l], conv_b[l][None, :],
               _block_diag(rg_w_a[l]).astype(BF16), rg_b_a[l][None, :],
               _block_diag(rg_w_x[l]).astype(BF16), rg_b_x[l][None, :], rg_lambda[l][None, :])
        mp = (w_br_sb[l].astype(BF16), w_br_rg[l].astype(BF16), w_br_df[l].astype(BF16),
              w_out[l].astype(BF16), ln_g[l][None, :], ln_b[l][None, :])
        lam_p = jnp.stack([lam_q1[l], lam_k1[l], lam_q2[l], lam_k2[l]])
        g_row = subln_g[l][None, :]
        g_col = subln_g[l][:, None]

        (q_sb, k_sb, v_sb, g_sb, x_rg, g_rg, q_df, k_df, v_df, g_df, gate) = _in_proj(
            xp, w_bf, b_l, cos_p, sin_p, BF16, tm_p)
        o_sb = _sb_prompt(q_sb, k_sb, v_sb, batch, seq, tq)
        o_df = _diff_prompt(q_df, k_df, v_df, lam_p, g_col, lam_init, batch, seq, tq)
        h_rg, h_last = _rglru_prompt(x_rg, zeros_buf, zeros_h, rgp, batch, seq, tc)
        xp = _merge(o_sb, g_sb, h_rg, g_rg, o_df, g_df, gate, xp, mp, alpha, tm_p)
        x_rg3 = x_rg.reshape(batch, seq, D_RNN)
        new_p.append((k_sb.reshape(batch, seq, H_SB, D_HEAD), v_sb.reshape(batch, seq, H_SB, D_HEAD),
                      k_df.reshape(batch, seq, H_DIFF, 2 * D_HEAD),
                      v_df.reshape(batch, seq, H_DIFF, 2 * D_HEAD),
                      jnp.concatenate([zeros_buf, x_rg3], axis=1)[:, seq:, :]
                      if seq < CONV_W - 1 else x_rg3[:, seq - (CONV_W - 1):, :],
                      h_last.reshape(batch, D_RNN)))

        (q_sb, k_sb, v_sb, g_sb, x_rg, g_rg, q_df, k_df, v_df, g_df, gate) = _in_proj(
            xs, w_bf, b_l, cos_s, sin_s, F32, m_s)
        tok3 = lambda a: a.reshape(nb, steps, W_ATT)
        o_sb = _decode_attention(
            _sb_decode_kernel, l, page_table, tok3(q_sb), tok3(k_sb), tok3(v_sb), c_sb_k, c_sb_v,
            [], [],
            [pltpu.VMEM((steps * N_GROUPS, W_ATT), F32), pltpu.VMEM((steps * N_GROUPS, 1), F32)],
            "sb_decode", True)
        o_df = _decode_attention(
            functools.partial(_diff_decode_kernel, lam_init=lam_init), l, page_table,
            tok3(q_df), tok3(k_df), tok3(v_df), c_df_k, c_df_v,
            [lam_p, g_row],
            [pl.BlockSpec((4, D_HEAD), lambda b, j, pt: (0, 0)),
             pl.BlockSpec((1, 2 * D_HEAD), lambda b, j, pt: (0, 0))],
            [pltpu.VMEM((steps * N_GROUPS, W_ATT), F32), pltpu.VMEM((steps * N_GROUPS, 1), F32),
             pltpu.VMEM((steps * N_GROUPS, 1), F32)],
            "diff_decode", False)
        x_rg3 = x_rg.reshape(nb, steps, D_RNN)
        h_tm, h_last = _rglru_decode(jnp.swapaxes(x_rg3, 0, 1), jnp.swapaxes(state_conv[l], 0, 1),
                                     state_rglru[l], rgp)
        h_rg = jnp.swapaxes(h_tm, 0, 1).reshape(m_s, D_RNN)
        xs = _merge(o_sb.reshape(m_s, W_ATT), g_sb, h_rg, g_rg, o_df.reshape(m_s, W_ATT), g_df, gate,
                    xs, mp, alpha, m_s)
        new_buf = jnp.concatenate([state_conv[l], x_rg3], axis=1)[:, steps:, :]
        new_s.append((k_sb.reshape(nb, steps, H_SB, D_HEAD), v_sb.reshape(nb, steps, H_SB, D_HEAD),
                      k_df.reshape(nb, steps, H_DIFF, 2 * D_HEAD),
                      v_df.reshape(nb, steps, H_DIFF, 2 * D_HEAD), new_buf, h_last))

    stack = lambda rows, idx: jnp.stack([r[idx] for r in rows])
    return ((xp.reshape(batch, seq, d_model), xs.reshape(nb, steps, d_model))
            + tuple(stack(new_p, i) for i in range(6))
            + tuple(stack(new_s, i) for i in range(6)))
```

```python
import functools
import math

import jax
import jax.numpy as jnp
from jax import lax
from jax.experimental import pallas as pl
from jax.experimental.pallas import tpu as pltpu

F32 = jnp.float32
BF16 = jnp.bfloat16

D_HEAD = 64
H_SB = 8
H_DIFF = 4
W_ATT = 512
D_RNN = 512
RG_C = 8.0
CONV_W = 4
ROT_DIM = D_HEAD // 4
ROPE_THETA = 500000.0
N_BRANCH = 3
LN_EPS = 1e-5
QK_SCALE = D_HEAD ** -0.5

LANES = 128
SUBLANES = 8
VMEM_LIMIT_BYTES = 56 * 1024 * 1024

NEG_BIG = -1e30


def _softplus(z):
    return jnp.maximum(z, 0.0) + jnp.log(1.0 + jnp.exp(-jnp.abs(z)))


def _sigmoid(z):
    return 1.0 / (1.0 + jnp.exp(-z))


def _split_bf16(x):
    hi = x.astype(BF16)
    lo = (x - hi.astype(F32)).astype(BF16)
    return hi, lo


def _params(*sem):
    return pltpu.CompilerParams(dimension_semantics=sem, vmem_limit_bytes=VMEM_LIMIT_BYTES)


def _rope_table_kernel(freq_ref, sgn_ref, cos_ref, sin_ref, *, pos0, period):
    rows = cos_ref.shape[0]
    r = lax.broadcasted_iota(jnp.int32, (rows, LANES), 0)
    pos = (pos0 + r % period).astype(F32)
    ang = pos * freq_ref[...]
    cos_ref[...] = jnp.cos(ang)
    sin_ref[...] = jnp.sin(ang) * sgn_ref[...]


def _rope_tables(rows, pos0, period):
    half = ROT_DIM // 2
    freqs = ROPE_THETA ** (-jnp.arange(half, dtype=F32) / half)
    d = jnp.arange(LANES) % D_HEAD
    freq_lane = jnp.where(d < ROT_DIM, freqs[d % half], 0.0).astype(F32)[None, :]
    sgn_lane = jnp.where(d < half, -1.0, jnp.where(d < ROT_DIM, 1.0, 0.0)).astype(F32)[None, :]
    return pl.pallas_call(
        functools.partial(_rope_table_kernel, pos0=pos0, period=period),
        out_shape=(jax.ShapeDtypeStruct((rows, LANES), F32),) * 2,
        name="rope_table",
    )(freq_lane, sgn_lane)


_COL_Q_SB, _COL_K_SB, _COL_V_SB, _COL_G_SB = 0, 1, 2, 3
_COL_X_RG, _COL_G_RG = 4, 5
_COL_Q_DF, _COL_K_DF, _COL_V_DF, _COL_G_DF = 6, 7, 8, 9
_COL_GATE = 10


def _project(x_ref, w_ref, b_ref):
    xb = x_ref[...].astype(BF16)

    def proj(col, sub=0, width=W_ATT):
        off = col * W_ATT + sub * width
        return (jnp.dot(xb, w_ref[:, off:off + width], preferred_element_type=F32)
                + b_ref[:, off:off + width])

    return proj


def _rope(h, cos, sin):
    d = lax.broadcasted_iota(jnp.int32, cos.shape, 1) % D_HEAD
    first_half = d < (ROT_DIM // 2)
    out = []
    for j in range(W_ATT // LANES):
        xj = h[:, j * LANES:(j + 1) * LANES]
        up = pltpu.roll(xj, LANES - ROT_DIM // 2, 1)
        dn = pltpu.roll(xj, ROT_DIM // 2, 1)
        out.append(xj * cos + jnp.where(first_half, up, dn) * sin)
    return out


def _write_gate(proj, gate):
    d_model = gate.shape[1] // N_BRANCH
    for j in range(N_BRANCH):
        gate[:, j * d_model:(j + 1) * d_model] = proj(_COL_GATE, j, d_model)


def _in_proj_decode_kernel(x_ref, w_ref, b_ref, cos_ref, sin_ref,
                           q_sb, k_sb, v_sb, g_sb, x_rg, g_rg, q_df, k_df, v_df, g_df, gate):
    proj = _project(x_ref, w_ref, b_ref)
    cos, sin = cos_ref[...], sin_ref[...]
    q_sb[...] = proj(_COL_Q_SB) * QK_SCALE
    k_sb[...] = proj(_COL_K_SB)
    v_sb[...] = proj(_COL_V_SB)
    g_sb[...] = proj(_COL_G_SB)
    x_rg[...] = proj(_COL_X_RG)
    g_rg[...] = proj(_COL_G_RG)
    for j, r in enumerate(_rope(proj(_COL_Q_DF), cos, sin)):
        q_df[:, j * LANES:(j + 1) * LANES] = r * QK_SCALE
    for j, r in enumerate(_rope(proj(_COL_K_DF), cos, sin)):
        k_df[:, j * LANES:(j + 1) * LANES] = r
    v_df[...] = proj(_COL_V_DF)
    g_df[...] = proj(_COL_G_DF)
    _write_gate(proj, gate)


_N_PROMPT_OUTPUTS = 15


def _in_proj_prompt_kernel(x_ref, w_ref, b_ref, cos_ref, sin_ref, *rest):
    (q_sb, k_sb, v_sb, kt_sb, vt_sb, g_sb, x_rg, g_rg,
     q_df, k_df, v_df, k4_df, v4_df, g_df, gate) = rest[len(rest) - _N_PROMPT_OUTPUTS:]
    tm = x_ref.shape[0]
    proj = _project(x_ref, w_ref, b_ref)
    cos, sin = cos_ref[...], sin_ref[...]
    q_sb[...] = (proj(_COL_Q_SB) * QK_SCALE).astype(BF16)
    k = proj(_COL_K_SB)
    k_sb[...] = k.astype(BF16)
    kt_sb[...] = k.T
    v = proj(_COL_V_SB)
    v_sb[...] = v.astype(BF16)
    vt_sb[...] = v.T
    g_sb[...] = proj(_COL_G_SB)
    x_rg[...] = proj(_COL_X_RG)
    g_rg[...] = proj(_COL_G_RG)
    for j, r in enumerate(_rope(proj(_COL_Q_DF), cos, sin)):
        q_df[:, j * LANES:(j + 1) * LANES] = (r * QK_SCALE).astype(BF16)
    for j, r in enumerate(_rope(proj(_COL_K_DF), cos, sin)):
        k_df[:, j * LANES:(j + 1) * LANES] = r.astype(BF16)
        k4_df[pl.ds(j, tm, stride=H_DIFF), :] = r
    v = proj(_COL_V_DF)
    v_df[...] = v.astype(BF16)
    for j in range(H_DIFF):
        v4_df[pl.ds(j, tm, stride=H_DIFF), :] = v[:, j * LANES:(j + 1) * LANES]
    g_df[...] = proj(_COL_G_DF)
    _write_gate(proj, gate)


def _in_proj_common_specs(x2d, w_bf, cos_t, tm):
    d_model = x2d.shape[1]
    d_in = w_bf.shape[1]
    n_t = cos_t.shape[0] // tm
    row = lambda i: (i, 0)
    const = lambda i: (0, 0)
    tab = lambda i: (i % n_t, 0)
    return [pl.BlockSpec((tm, d_model), row),
            pl.BlockSpec((d_model, d_in), const, pipeline_mode=pl.Buffered(1)),
            pl.BlockSpec((1, d_in), const, pipeline_mode=pl.Buffered(1)),
            pl.BlockSpec((tm, LANES), tab),
            pl.BlockSpec((tm, LANES), tab)]


def _in_proj_decode(x2d, w_bf, b, cos_t, sin_t):
    m, d_model = x2d.shape
    row = lambda i: (i, 0)
    att = jax.ShapeDtypeStruct((m, W_ATT), F32)
    out_shape = (att,) * 10 + (jax.ShapeDtypeStruct((m, N_BRANCH * d_model), F32),)
    return pl.pallas_call(
        _in_proj_decode_kernel,
        grid=(1,),
        in_specs=_in_proj_common_specs(x2d, w_bf, cos_t, m),
        out_specs=tuple(pl.BlockSpec((m, s.shape[1]), row) for s in out_shape),
        out_shape=out_shape,
        compiler_params=_params("arbitrary"),
        name="in_proj_decode",
    )(x2d, w_bf, b, cos_t, sin_t)


def _in_proj_prompt(x2d, w_bf, b, cos_t, sin_t, stacked, layer, depth, batch, seq, tm):
    m, d_model = x2d.shape
    n_t = seq // tm
    row = lambda i: (i, 0)
    act = lambda dt: jax.ShapeDtypeStruct((m, W_ATT), dt)
    t_shape = jax.ShapeDtypeStruct((depth, batch, W_ATT, seq), F32)
    r_shape = jax.ShapeDtypeStruct((depth, m * H_DIFF, LANES), F32)
    t_spec = pl.BlockSpec((None, None, W_ATT, tm), lambda i: (layer, i // n_t, 0, i % n_t))
    r_spec = pl.BlockSpec((None, tm * H_DIFF, LANES), lambda i: (layer, i, 0))
    a_spec = pl.BlockSpec((tm, W_ATT), row)
    out_shape = (act(BF16), act(BF16), act(BF16), t_shape, t_shape, act(F32), act(F32), act(F32),
                 act(BF16), act(BF16), act(BF16), r_shape, r_shape, act(F32),
                 jax.ShapeDtypeStruct((m, N_BRANCH * d_model), F32))
    out_specs = ((a_spec,) * 3 + (t_spec, t_spec) + (a_spec,) * 6
                 + (r_spec, r_spec, a_spec, pl.BlockSpec((tm, N_BRANCH * d_model), row)))
    assert len(out_shape) == _N_PROMPT_OUTPUTS
    in_specs = _in_proj_common_specs(x2d, w_bf, cos_t, tm)
    args = [x2d, w_bf, b, cos_t, sin_t]
    aliases = {}
    if stacked is not None:
        for arr, out_idx in zip(stacked, (3, 4, 11, 12)):
            aliases[len(args)] = out_idx
            in_specs.append(pl.BlockSpec(memory_space=pl.ANY))
            args.append(arr)
    return pl.pallas_call(
        _in_proj_prompt_kernel,
        grid=(m // tm,),
        in_specs=in_specs,
        out_specs=out_specs,
        out_shape=out_shape,
        input_output_aliases=aliases,
        compiler_params=_params("parallel"),
        name="in_proj_prompt",
    )(*args)


KEY_BLOCKS_PER_ITER = 2

_NT = (((1,), (1,)), ((), ()))
_TN = (((0,), (0,)), ((), ()))


def _stacked_queries(q):
    lane = lax.broadcasted_iota(jnp.int32, q.shape, 1)
    zero = jnp.zeros_like(q)
    return jnp.concatenate([jnp.where(lane < D_HEAD, q, zero), jnp.where(lane >= D_HEAD, q, zero)],
                           axis=0)


def _diag_mask(tk, tq, offset, strict):
    s_idx = lax.broadcasted_iota(jnp.int32, (tk, 2 * tq), 0) + offset
    t_idx = lax.broadcasted_iota(jnp.int32, (tk, 2 * tq), 1) % tq
    return (s_idx < t_idx) if strict else (s_idx <= t_idx)


def _sweep(i, ratio, first, single, group, group_size, state):
    for d in reversed(range(ratio)):
        state = first(i * ratio + d, d, state)
    n = i * ratio
    rem = n % group_size
    state = lax.fori_loop(0, rem, lambda j, s: single(n - 1 - j, s), state)
    base = n - rem
    return lax.fori_loop(0, base // group_size, lambda j, s: group(base - 1 - group_size * j, s), state)


def _sb_prompt_kernel(q_ref, k_ref, v_ref, o_ref, acc_ref, *, tq, tk):
    i = pl.program_id(2)
    q2 = _stacked_queries(q_ref[...])
    s_idx = lax.broadcasted_iota(jnp.int32, (tk, 2 * tk), 0)
    j_idx = lax.broadcasted_iota(jnp.int32, (tk, 2 * tk), 1) % tk
    later_keys = (j_idx > s_idx).astype(BF16)

    acc_ref[...] = jnp.zeros_like(acc_ref)

    def chain(kb, carry, diag):
        start = pl.multiple_of(kb * tk, tk)
        k = k_ref[pl.ds(start, tk), :]
        v = v_ref[pl.ds(start, tk), :]
        z = lax.dot_general(k, q2, _NT, preferred_element_type=F32)
        sp = _softplus(z)
        if diag is not None:
            valid = _diag_mask(tk, tq, diag * tk, True)
            sp = jnp.where(valid, sp, 0.0)
        hi, lo = _split_bf16(sp)
        later = jnp.dot(later_keys, jnp.concatenate([hi, lo], axis=0),
                        preferred_element_type=F32)
        w = jnp.exp(z - sp - later + carry)
        if diag is not None:
            w = jnp.where(valid, w, 0.0)
        pv = lax.dot_general(v, w.astype(BF16), _TN, preferred_element_type=F32)
        return pv, carry - (later[0:1, :] + sp[0:1, :])

    def first(kb, d, carry):
        pv, carry = chain(kb, carry, d)
        acc_ref[...] += pv
        return carry

    def single(kb, carry):
        pv, carry = chain(kb, carry, None)
        acc_ref[...] += pv
        return carry

    def group(kb, carry):
        total = None
        for g in range(KEY_BLOCKS_PER_ITER):
            pv, carry = chain(kb - g, carry, None)
            total = pv if total is None else total + pv
        acc_ref[...] += total
        return carry

    _sweep(i, tq // tk, first, single, group, KEY_BLOCKS_PER_ITER, jnp.zeros((1, 2 * tq), F32))

    o_t = jnp.concatenate([acc_ref[0:D_HEAD, 0:tq], acc_ref[D_HEAD:2 * D_HEAD, tq:2 * tq]], axis=0)
    o_ref[...] = o_t.T


def _sb_prompt(q, k, v, batch, seq, tq, tk):
    m = q.shape[0]
    nq = seq // tq
    hp = H_SB // 2
    return pl.pallas_call(
        functools.partial(_sb_prompt_kernel, tq=tq, tk=tk),
        grid=(batch, hp, nq),
        in_specs=[pl.BlockSpec((tq, LANES), lambda b, h, i: (b * nq + i, h)),
                  pl.BlockSpec((seq, LANES), lambda b, h, i: (b, h)),
                  pl.BlockSpec((seq, LANES), lambda b, h, i: (b, h))],
        out_specs=pl.BlockSpec((tq, LANES), lambda b, h, i: (b * nq + i, h)),
        out_shape=jax.ShapeDtypeStruct((m, W_ATT), F32),
        scratch_shapes=[pltpu.VMEM((LANES, 2 * tq), F32)],
        compiler_params=_params("parallel", "parallel", "parallel"),
        name="sb_prompt",
    )(q, k, v)


def _lambda(lam_ref, lam_init):
    p = lam_ref[...]
    s1 = jnp.sum(p[0:1, :] * p[1:2, :], axis=-1, keepdims=True)
    s2 = jnp.sum(p[2:3, :] * p[3:4, :], axis=-1, keepdims=True)
    return jnp.exp(s1) - jnp.exp(s2) + lam_init


def _diff_prompt_kernel(q_ref, k_ref, v_ref, lam_ref, g_ref, o_ref, acc_ref, *, tq, tk, lam_init):
    i = pl.program_id(2)
    q2 = _stacked_queries(q_ref[...])

    acc_ref[...] = jnp.zeros_like(acc_ref)

    def scores(kb, diag):
        start = pl.multiple_of(kb * tk, tk)
        k = k_ref[pl.ds(start, tk), :]
        z = lax.dot_general(k, q2, _NT, preferred_element_type=F32)
        if diag is not None:
            z = jnp.where(_diag_mask(tk, tq, diag * tk, False), z, NEG_BIG)
        return z

    def update(kbs, zs, state):
        m_old, l_old = state
        m_new = m_old
        for z in zs:
            m_new = jnp.maximum(m_new, jnp.max(z, axis=0, keepdims=True))
        alpha = jnp.exp(m_old - m_new)
        l_new = alpha * l_old
        pv = None
        for kb, z in zip(kbs, zs):
            p = jnp.exp(z - m_new)
            l_new = l_new + jnp.sum(p, axis=0, keepdims=True)
            v = v_ref[pl.ds(pl.multiple_of(kb * tk, tk), tk), :]
            term = lax.dot_general(v, p.astype(BF16), _TN, preferred_element_type=F32)
            pv = term if pv is None else pv + term
        acc_ref[...] = alpha * acc_ref[...] + pv
        return m_new, l_new

    def first(kb, d, state):
        return update((kb,), (scores(kb, d),), state)

    def single(kb, state):
        return update((kb,), (scores(kb, None),), state)

    def group(kb, state):
        kbs = tuple(kb - g for g in range(KEY_BLOCKS_PER_ITER))
        return update(kbs, tuple(scores(b, None) for b in kbs), state)

    init = (jnp.full((1, 2 * tq), NEG_BIG, F32), jnp.zeros((1, 2 * tq), F32))
    _, l = _sweep(i, tq // tk, first, single, group, KEY_BLOCKS_PER_ITER, init)

    lam = _lambda(lam_ref, lam_init)
    o_n = acc_ref[...] / l
    o_t = o_n[:, 0:tq] - lam * o_n[:, tq:2 * tq]
    ms = jnp.mean(o_t * o_t, axis=0, keepdims=True)
    o_t = o_t * lax.rsqrt(ms + LN_EPS) * g_ref[...] * (1.0 - lam_init)
    o_ref[...] = o_t.T


def _diff_prompt(q, k, v, lam_p, g_col, lam_init, batch, seq, tq, tk):
    m = q.shape[0]
    nq = seq // tq
    return pl.pallas_call(
        functools.partial(_diff_prompt_kernel, tq=tq, tk=tk, lam_init=lam_init),
        grid=(batch, H_DIFF, nq),
        in_specs=[pl.BlockSpec((tq, LANES), lambda b, h, i: (b * nq + i, h)),
                  pl.BlockSpec((seq, LANES), lambda b, h, i: (b, h)),
                  pl.BlockSpec((seq, LANES), lambda b, h, i: (b, h)),
                  pl.BlockSpec((4, D_HEAD), lambda b, h, i: (0, 0)),
                  pl.BlockSpec((2 * D_HEAD, 1), lambda b, h, i: (0, 0))],
        out_specs=pl.BlockSpec((tq, LANES), lambda b, h, i: (b * nq + i, h)),
        out_shape=jax.ShapeDtypeStruct((m, W_ATT), F32),
        scratch_shapes=[pltpu.VMEM((LANES, 2 * tq), F32)],
        compiler_params=_params("parallel", "parallel", "parallel"),
        name="diff_prompt",
    )(q, k, v, lam_p, g_col)


def _rg_gates(xc, wa_ref, ba_ref, wx_ref, bx_ref, lam_ref):
    xb = xc.astype(BF16)
    r = _sigmoid(jnp.dot(xb, wa_ref[...], preferred_element_type=F32) + ba_ref[...])
    g = _sigmoid(jnp.dot(xb, wx_ref[...], preferred_element_type=F32) + bx_ref[...])
    log_a = (-RG_C * _softplus(-lam_ref[...])) * r
    a = jnp.exp(log_a)
    t = jnp.tanh(log_a)
    mult = jnp.sqrt(-2.0 * t / (1.0 - t))
    return a, mult * (g * xc)


def _rglru_prompt_kernel(x_ref, buf0_ref, h0_ref, cw_ref, cb_ref, wa_ref, ba_ref, wx_ref, bx_ref,
                         lam_ref, h_ref, hlast_ref, xbuf, hstate, *, tc):
    c = pl.program_id(1)
    pad = SUBLANES

    @pl.when(c == 0)
    def _():
        xbuf[0:pad, :] = jnp.zeros((pad, D_RNN), F32)
        xbuf[pad - (CONV_W - 1):pad, :] = buf0_ref[...]
        hstate[...] = h0_ref[...]

    @pl.when(c != 0)
    def _():
        xbuf[0:pad, :] = xbuf[tc:tc + pad, :]

    xbuf[pad:pad + tc, :] = x_ref[...]
    xc = cb_ref[...]
    for kk in range(CONV_W):
        off = pad - (CONV_W - 1) + kk
        xc = xc + xbuf[off:off + tc, :] * cw_ref[kk:kk + 1, :]

    a, b = _rg_gates(xc, wa_ref, ba_ref, wx_ref, bx_ref, lam_ref)

    row = lax.broadcasted_iota(jnp.int32, a.shape, 0)
    dist = 1
    while dist < tc:
        keep = row >= dist
        a_sh = pltpu.roll(a, dist, 0)
        b_sh = pltpu.roll(b, dist, 0)
        b = jnp.where(keep, a * b_sh + b, b)
        a = jnp.where(keep, a * a_sh, a)
        dist *= 2
    h = b + a * hstate[...]
    h_ref[...] = h
    hstate[...] = h[tc - 1:tc, :]

    @pl.when(c == pl.num_programs(1) - 1)
    def _():
        hlast_ref[...] = h[tc - 1:tc, :]


def _rglru_prompt(x_rg, buf0, h0, rgp, batch, seq, tc):
    m = x_rg.shape[0]
    nc = seq // tc
    const = lambda b, c: (0, 0)
    wspec = lambda shape: pl.BlockSpec(shape, const)
    return pl.pallas_call(
        functools.partial(_rglru_prompt_kernel, tc=tc),
        grid=(batch, nc),
        in_specs=[pl.BlockSpec((tc, D_RNN), lambda b, c: (b * nc + c, 0)),
                  pl.BlockSpec((None, CONV_W - 1, D_RNN), lambda b, c: (b, 0, 0)),
                  pl.BlockSpec((None, 1, D_RNN), lambda b, c: (b, 0, 0)),
                  wspec((CONV_W, D_RNN)), wspec((1, D_RNN)),
                  wspec((D_RNN, D_RNN)), wspec((1, D_RNN)),
                  wspec((D_RNN, D_RNN)), wspec((1, D_RNN)), wspec((1, D_RNN))],
        out_specs=(pl.BlockSpec((tc, D_RNN), lambda b, c: (b * nc + c, 0)),
                   pl.BlockSpec((None, 1, D_RNN), lambda b, c: (b, 0, 0))),
        out_shape=(jax.ShapeDtypeStruct((m, D_RNN), F32),
                   jax.ShapeDtypeStruct((batch, 1, D_RNN), F32)),
        scratch_shapes=[pltpu.VMEM((tc + SUBLANES, D_RNN), F32), pltpu.VMEM((1, D_RNN), F32)],
        compiler_params=_params("parallel", "arbitrary"),
        name="rglru_prompt",
    )(x_rg, buf0, h0, *rgp)


def _rglru_decode_kernel(x_ref, buf_ref, h0_ref, cw_ref, cb_ref, wa_ref, ba_ref, wx_ref, bx_ref,
                         lam_ref, h_ref, hlast_ref):
    steps = x_ref.shape[0]
    rows = [buf_ref[j] for j in range(CONV_W - 1)] + [x_ref[t] for t in range(steps)]
    h = h0_ref[...]
    for t in range(steps):
        xc = cb_ref[...]
        for kk in range(CONV_W):
            xc = xc + rows[t + kk] * cw_ref[kk:kk + 1, :]
        a, b = _rg_gates(xc, wa_ref, ba_ref, wx_ref, bx_ref, lam_ref)
        h = a * h + b
        h_ref[t] = h
    hlast_ref[...] = h


def _rglru_decode(x_tm, buf_tm, h0, rgp):
    steps, nb, _ = x_tm.shape
    return pl.pallas_call(
        _rglru_decode_kernel,
        out_shape=(jax.ShapeDtypeStruct((steps, nb, D_RNN), F32),
                   jax.ShapeDtypeStruct((nb, D_RNN), F32)),
        compiler_params=pltpu.CompilerParams(vmem_limit_bytes=VMEM_LIMIT_BYTES),
        name="rglru_decode",
    )(x_tm, buf_tm, h0, *rgp)


def _merge_kernel(o_sb, g_sb, h_rg, g_rg, o_df, g_df, gate, x_ref,
                  w_sb, w_rg, w_df, w_out, lng, lnb, y_ref, *, alpha):
    def branch(o_ref, g_ref, w_ref):
        g = g_ref[...]
        act = o_ref[...] * (g * _sigmoid(g))
        return jnp.dot(act.astype(BF16), w_ref[...], preferred_element_type=F32)

    d_model = x_ref.shape[1]
    merged = None
    for j, (o_r, g_r, w_r) in enumerate(((o_sb, g_sb, w_sb), (h_rg, g_rg, w_rg), (o_df, g_df, w_df))):
        term = _sigmoid(gate[:, j * d_model:(j + 1) * d_model]) * branch(o_r, g_r, w_r)
        merged = term if merged is None else merged + term
    z = alpha * x_ref[...] + jnp.dot(merged.astype(BF16), w_out[...], preferred_element_type=F32)
    mu = jnp.mean(z, axis=-1, keepdims=True)
    zc = z - mu
    var = jnp.mean(zc * zc, axis=-1, keepdims=True)
    y_ref[...] = zc * lax.rsqrt(var + LN_EPS) * lng[...] + lnb[...]


def _merge(o_sb, g_sb, h_rg, g_rg, o_df, g_df, gate, x2d, mp, alpha, tm):
    m, d_model = x2d.shape
    row = lambda i: (i, 0)
    const = lambda i: (0, 0)
    act = pl.BlockSpec((tm, W_ATT), row)
    return pl.pallas_call(
        functools.partial(_merge_kernel, alpha=alpha),
        grid=(m // tm,),
        in_specs=[act] * 6 + [pl.BlockSpec((tm, N_BRANCH * d_model), row),
                              pl.BlockSpec((tm, d_model), row)]
                 + [pl.BlockSpec((W_ATT, d_model), const)] * 3
                 + [pl.BlockSpec((d_model, d_model), const),
                    pl.BlockSpec((1, d_model), const), pl.BlockSpec((1, d_model), const)],
        out_specs=pl.BlockSpec((tm, d_model), row),
        out_shape=jax.ShapeDtypeStruct((m, d_model), F32),
        compiler_params=_params("parallel"),
        name="merge",
    )(o_sb, g_sb, h_rg, g_rg, o_df, g_df, gate, x2d, *mp)


PAGES_PER_STEP = 16
KEY_BLOCK = 256
N_GROUPS = W_ATT // D_HEAD


def _rows_per_token(x, lo, hi):
    steps = x.shape[0]
    return jnp.concatenate(
        [jnp.broadcast_to(x[t:t + 1, lo:hi], (N_GROUPS, hi - lo)) for t in range(steps)], axis=0)


def _sb_decode_kernel(pt_ref, q_ref, kn_ref, vn_ref, *rest):
    del pt_ref
    k_pages = rest[:PAGES_PER_STEP]
    v_pages = rest[PAGES_PER_STEP:2 * PAGES_PER_STEP]
    o_ref, acc_ref, carry_ref = rest[2 * PAGES_PER_STEP:]
    j = pl.program_id(1)
    steps = q_ref.shape[0]
    rep = _rows_per_token(q_ref[...], 0, W_ATT)
    row = lax.broadcasted_iota(jnp.int32, rep.shape, 0)
    lane = lax.broadcasted_iota(jnp.int32, rep.shape, 1)
    own = lane // D_HEAD == row % N_GROUPS
    qx = jnp.where(own, rep, 0.0)
    rows = qx.shape[0]
    tok = lax.broadcasted_iota(jnp.int32, (rows, 1), 0) // N_GROUPS

    @pl.when(j == 0)
    def _():
        z, sp, valid = [], [], []
        for jn in range(steps):
            zj = jnp.sum(qx * kn_ref[jn:jn + 1, :], axis=-1, keepdims=True)
            vj = tok > jn
            z.append(zj)
            valid.append(vj)
            sp.append(jnp.where(vj, _softplus(zj), 0.0))
        carry = jnp.zeros((rows, 1), F32)
        acc = jnp.zeros((rows, W_ATT), F32)
        for jn in reversed(range(steps)):
            w = jnp.where(valid[jn], jnp.exp(z[jn] - sp[jn] + carry), 0.0)
            acc = acc + w * vn_ref[jn:jn + 1, :]
            carry = carry - sp[jn]
        acc_ref[...] = acc
        carry_ref[...] = carry

    qb = qx.astype(BF16)
    j_idx = lax.broadcasted_iota(jnp.int32, (2 * KEY_BLOCK, KEY_BLOCK), 0) % KEY_BLOCK
    s_idx = lax.broadcasted_iota(jnp.int32, (2 * KEY_BLOCK, KEY_BLOCK), 1)
    later_keys = (j_idx > s_idx).astype(BF16)
    keys = PAGES_PER_STEP * k_pages[0].shape[1]
    n_blocks = keys // KEY_BLOCK
    blk = lambda x, b: x[:, b * KEY_BLOCK:(b + 1) * KEY_BLOCK]
    kt = jnp.concatenate([r[...] for r in k_pages], axis=1).astype(BF16)
    z = jnp.dot(qb, kt, preferred_element_type=F32)
    sp = _softplus(z)
    hi, lo = _split_bf16(sp)
    later = jnp.concatenate(
        [jnp.dot(jnp.concatenate([blk(hi, b), blk(lo, b)], axis=1), later_keys,
                 preferred_element_type=F32) for b in range(n_blocks)], axis=1)
    carry = carry_ref[...]
    carries = [None] * n_blocks
    for b in reversed(range(n_blocks)):
        carries[b] = jnp.broadcast_to(carry, (rows, KEY_BLOCK))
        first = b * KEY_BLOCK
        carry = carry - (later[:, first:first + 1] + sp[:, first:first + 1])
    w = jnp.exp(z - sp - later + jnp.concatenate(carries, axis=1))
    vt = jnp.concatenate([r[...] for r in v_pages], axis=1).astype(BF16)
    acc = acc_ref[...] + lax.dot_general(w.astype(BF16), vt, _NT, preferred_element_type=F32)
    acc_ref[...] = acc
    carry_ref[...] = carry

    @pl.when(j == pl.num_programs(1) - 1)
    def _():
        o_ref[...] = jnp.sum(jnp.where(own, acc, 0.0).reshape(steps, N_GROUPS, W_ATT), axis=1)


def _diff_decode_kernel(pt_ref, q_ref, kn_ref, vn_ref, lam_ref, g_ref, *rest, lam_init):
    del pt_ref
    k_pages = rest[:PAGES_PER_STEP]
    v_pages = rest[PAGES_PER_STEP:2 * PAGES_PER_STEP]
    o_ref, acc_ref, m_ref, l_ref = rest[2 * PAGES_PER_STEP:]
    j = pl.program_id(1)
    steps = q_ref.shape[0]
    rows = steps * N_GROUPS
    row = lax.broadcasted_iota(jnp.int32, (rows, LANES), 0)
    lane = lax.broadcasted_iota(jnp.int32, (rows, LANES), 1)
    head = (row % N_GROUPS) // 2
    comp = row % 2

    def per_head(x):
        out = jnp.zeros((rows, LANES), F32)
        for h in range(H_DIFF):
            out = jnp.where(head == h, _rows_per_token(x, h * LANES, (h + 1) * LANES), out)
        return out

    def per_head_row(x_row):
        out = jnp.zeros((rows, LANES), F32)
        for h in range(H_DIFF):
            out = jnp.where(head == h,
                            jnp.broadcast_to(x_row[:, h * LANES:(h + 1) * LANES], (rows, LANES)), out)
        return out

    qx = jnp.where((lane // D_HEAD) == comp, per_head(q_ref[...]), 0.0)
    tok = lax.broadcasted_iota(jnp.int32, (rows, 1), 0) // N_GROUPS

    @pl.when(j == 0)
    def _():
        z = []
        m = jnp.full((rows, 1), NEG_BIG, F32)
        for jn in range(steps):
            zj = jnp.sum(qx * per_head_row(kn_ref[jn:jn + 1, :]), axis=-1, keepdims=True)
            zj = jnp.where(tok >= jn, zj, NEG_BIG)
            z.append(zj)
            m = jnp.maximum(m, zj)
        l = jnp.zeros((rows, 1), F32)
        acc = jnp.zeros((rows, LANES), F32)
        for jn in range(steps):
            p = jnp.exp(z[jn] - m)
            l = l + p
            acc = acc + p * per_head_row(vn_ref[jn:jn + 1, :])
        acc_ref[...] = acc
        m_ref[...] = m
        l_ref[...] = l

    qb = qx.astype(BF16)
    n_cols = PAGES_PER_STEP * k_pages[0].shape[0]
    col_head = lax.broadcasted_iota(jnp.int32, (rows, n_cols), 1) % H_DIFF
    row_head = (lax.broadcasted_iota(jnp.int32, (rows, n_cols), 0) % N_GROUPS) // 2
    kb = jnp.concatenate([r[...] for r in k_pages], axis=0).astype(BF16)
    z = lax.dot_general(qb, kb, _NT, preferred_element_type=F32)
    z = jnp.where(col_head == row_head, z, NEG_BIG)
    m_old = m_ref[...]
    m_new = jnp.maximum(m_old, jnp.max(z, axis=-1, keepdims=True))
    alpha = jnp.exp(m_old - m_new)
    p = jnp.exp(z - m_new)
    l = alpha * l_ref[...] + jnp.sum(p, axis=-1, keepdims=True)
    vb = jnp.concatenate([r[...] for r in v_pages], axis=0).astype(BF16)
    acc = alpha * acc_ref[...] + jnp.dot(p.astype(BF16), vb, preferred_element_type=F32)
    acc_ref[...] = acc
    m_ref[...] = m_new
    l_ref[...] = l

    @pl.when(j == pl.num_programs(1) - 1)
    def _():
        lam = _lambda(lam_ref, lam_init)
        comp1 = lax.broadcasted_iota(jnp.int32, (rows, 1), 0) % 2
        scaled = acc * (jnp.where(comp1 == 0, 1.0, -lam) / l)
        o = scaled + pltpu.roll(scaled, rows - 1, 0)
        ms = jnp.mean(o * o, axis=-1, keepdims=True)
        o_ref[...] = o * lax.rsqrt(ms + LN_EPS) * g_ref[...] * (1.0 - lam_init)


def _decode_attention(kernel_fn, layer, page_table, q3, kn3, vn3, cache_k, cache_v, extra, extra_specs,
                      out_cols, scratch, name, newest_first):
    nb, steps, _ = q3.shape
    n_pages = page_table.shape[1]
    n_chunks = n_pages // PAGES_PER_STEP
    page_shape = cache_k.shape[2:]

    def page_spec(p):
        def index(b, j, pt):
            chunk = (n_chunks - 1 - j) if newest_first else j
            return (layer, pt[b, chunk * PAGES_PER_STEP + p], 0, 0)
        return pl.BlockSpec((None, None) + page_shape, index)

    tok_spec = pl.BlockSpec((None, steps, W_ATT), lambda b, j, pt: (b, 0, 0))
    out_rows = steps if out_cols == W_ATT else steps * N_GROUPS
    grid_spec = pltpu.PrefetchScalarGridSpec(
        num_scalar_prefetch=1,
        grid=(nb, n_chunks),
        in_specs=[tok_spec] * 3 + extra_specs + [page_spec(p) for p in range(PAGES_PER_STEP)] * 2,
        out_specs=pl.BlockSpec((None, out_rows, out_cols), lambda b, j, pt: (b, 0, 0)),
        scratch_shapes=scratch,
    )
    return pl.pallas_call(
        kernel_fn,
        grid_spec=grid_spec,
        out_shape=jax.ShapeDtypeStruct((nb, out_rows, out_cols), F32),
        compiler_params=_params("parallel", "arbitrary"),
        name=name,
    )(page_table, q3, kn3, vn3, *extra, *([cache_k] * PAGES_PER_STEP), *([cache_v] * PAGES_PER_STEP))


def _block_diag(w):
    n, bi, bj = w.shape
    eye = jnp.eye(n, dtype=w.dtype)
    return (eye[:, None, :, None] * w[:, :, None, :]).reshape(n * bi, n * bj)


def _row_tile(m, want):
    return want if m % want == 0 else m


def kernel(x_prompt, x_sample, cache_sb_k, cache_sb_v, cache_diff_k, cache_diff_v, state_conv,
           state_rglru, page_table, w_in, b_in, conv_w, conv_b, rg_w_a, rg_b_a, rg_w_x, rg_b_x,
           rg_lambda, lam_q1, lam_k1, lam_q2, lam_k2, subln_g, w_br_sb, w_br_rg, w_br_df, w_out,
           ln_g, ln_b):
    depth = w_in.shape[0]
    batch, seq, d_model = x_prompt.shape
    nb, steps, _ = x_sample.shape
    n_pool, page_rows = cache_sb_k.shape[1], cache_sb_k.shape[2]
    past_len = page_table.shape[1] * page_rows
    alpha = (2 * depth) ** 0.25

    assert cache_sb_k.shape[3:] == (H_SB, D_HEAD) and cache_diff_k.shape[3:] == (H_DIFF, 2 * D_HEAD)
    assert page_table.shape[1] % PAGES_PER_STEP == 0 and KEY_BLOCK % page_rows == 0

    tm_p = _row_tile(seq, 256)
    tq = _row_tile(seq, 512)
    tk = _row_tile(tq, 256)
    tc = _row_tile(seq, 256)
    m_s = nb * steps

    cos_p, sin_p = _rope_tables(seq, 0, seq)
    cos_s, sin_s = _rope_tables(m_s, past_len, steps)

    sb_pages = lambda c: jnp.transpose(c, (0, 1, 3, 4, 2)).reshape(depth, n_pool, W_ATT, page_rows)
    df_pages = lambda c: c.reshape(depth, n_pool, page_rows * H_DIFF, 2 * D_HEAD)
    c_sb_k, c_sb_v = sb_pages(cache_sb_k), sb_pages(cache_sb_v)
    c_df_k, c_df_v = df_pages(cache_diff_k), df_pages(cache_diff_v)

    xp = x_prompt.reshape(batch * seq, d_model)
    xs = x_sample.reshape(m_s, d_model)
    zeros_buf = jnp.zeros((batch, CONV_W - 1, D_RNN), F32)
    zeros_h = jnp.zeros((batch, 1, D_RNN), F32)

    stacked = None
    p_conv, p_h, new_s = [], [], []
    for l in range(depth):
        lam_init = 0.8 - 0.6 * math.exp(-0.3 * l)
        w_bf = w_in[l].astype(BF16)
        b_l = b_in[l][None, :]
        rgp = (conv_w[l], conv_b[l][None, :],
               _block_diag(rg_w_a[l]).astype(BF16), rg_b_a[l][None, :],
               _block_diag(rg_w_x[l]).astype(BF16), rg_b_x[l][None, :], rg_lambda[l][None, :])
        mp = (w_br_sb[l].astype(BF16), w_br_rg[l].astype(BF16), w_br_df[l].astype(BF16),
              w_out[l].astype(BF16), ln_g[l][None, :], ln_b[l][None, :])
        lam_p = jnp.stack([lam_q1[l], lam_k1[l], lam_q2[l], lam_k2[l]])
        g_row = subln_g[l][None, :]
        g_col = subln_g[l][:, None]

        (q_sb, k_sb, v_sb, kt_sb, vt_sb, g_sb, x_rg, g_rg, q_df, k_df, v_df, k4_df, v4_df, g_df,
         gate) = _in_proj_prompt(xp, w_bf, b_l, cos_p, sin_p, stacked, l, depth, batch, seq, tm_p)
        stacked = (kt_sb, vt_sb, k4_df, v4_df)
        o_sb = _sb_prompt(q_sb, k_sb, v_sb, batch, seq, tq, tk)
        o_df = _diff_prompt(q_df, k_df, v_df, lam_p, g_col, lam_init, batch, seq, tq, tk)
        h_rg, h_last = _rglru_prompt(x_rg, zeros_buf, zeros_h, rgp, batch, seq, tc)
        xp = _merge(o_sb, g_sb, h_rg, g_rg, o_df, g_df, gate, xp, mp, alpha, tm_p)
        x_rg3 = x_rg.reshape(batch, seq, D_RNN)
        p_conv.append(jnp.concatenate([zeros_buf, x_rg3], axis=1)[:, seq:, :])
        p_h.append(h_last.reshape(batch, D_RNN))

        (q_sb, k_sb, v_sb, g_sb, x_rg, g_rg, q_df, k_df, v_df, g_df, gate) = _in_proj_decode(
            xs, w_bf, b_l, cos_s, sin_s)
        tok3 = lambda a: a.reshape(nb, steps, W_ATT)
        rows = steps * N_GROUPS
        o_sb = _decode_attention(
            _sb_decode_kernel, l, page_table, tok3(q_sb), tok3(k_sb), tok3(v_sb), c_sb_k, c_sb_v,
            [], [], W_ATT,
            [pltpu.VMEM((rows, W_ATT), F32), pltpu.VMEM((rows, 1), F32)],
            "sb_decode", True)
        o_df = _decode_attention(
            functools.partial(_diff_decode_kernel, lam_init=lam_init), l, page_table,
            tok3(q_df), tok3(k_df), tok3(v_df), c_df_k, c_df_v,
            [lam_p, g_row],
            [pl.BlockSpec((4, D_HEAD), lambda b, j, pt: (0, 0)),
             pl.BlockSpec((1, 2 * D_HEAD), lambda b, j, pt: (0, 0))], LANES,
            [pltpu.VMEM((rows, LANES), F32), pltpu.VMEM((rows, 1), F32), pltpu.VMEM((rows, 1), F32)],
            "diff_decode", False)
        o_df = o_df[:, ::2, :].reshape(m_s, W_ATT)
        x_rg3 = x_rg.reshape(nb, steps, D_RNN)
        h_tm, h_last = _rglru_decode(jnp.swapaxes(x_rg3, 0, 1), jnp.swapaxes(state_conv[l], 0, 1),
                                     state_rglru[l], rgp)
        h_rg = jnp.swapaxes(h_tm, 0, 1).reshape(m_s, D_RNN)
        xs = _merge(o_sb.reshape(m_s, W_ATT), g_sb, h_rg, g_rg, o_df, g_df, gate, xs, mp, alpha, m_s)
        new_buf = jnp.concatenate([state_conv[l], x_rg3], axis=1)[:, steps:, :]
        new_s.append((k_sb.reshape(nb, steps, H_SB, D_HEAD), v_sb.reshape(nb, steps, H_SB, D_HEAD),
                      k_df.reshape(nb, steps, H_DIFF, 2 * D_HEAD),
                      v_df.reshape(nb, steps, H_DIFF, 2 * D_HEAD), new_buf, h_last))

    kt_sb, vt_sb, k4_df, v4_df = stacked
    sb_rows = lambda a: jnp.transpose(a.reshape(depth, batch, H_SB, D_HEAD, seq), (0, 1, 4, 2, 3))
    df_rows = lambda a: a.reshape(depth, batch, seq, H_DIFF, 2 * D_HEAD)
    stack = lambda rows_, idx: jnp.stack([r[idx] for r in rows_])
    return ((xp.reshape(batch, seq, d_model), xs.reshape(nb, steps, d_model),
             sb_rows(kt_sb), sb_rows(vt_sb), df_rows(k4_df), df_rows(v4_df),
             jnp.stack(p_conv), jnp.stack(p_h))
            + tuple(stack(new_s, i) for i in range(6)))
```

```python
import functools
import math

import jax
import jax.numpy as jnp
from jax import lax
from jax.experimental import pallas as pl
from jax.experimental.pallas import tpu as pltpu

F32 = jnp.float32
BF16 = jnp.bfloat16

D_HEAD = 64
H_SB = 8
H_DIFF = 4
W_ATT = 512
D_RNN = 512
RG_C = 8.0
CONV_W = 4
ROT_DIM = D_HEAD // 4
ROPE_THETA = 500000.0
N_BRANCH = 3
LN_EPS = 1e-5
QK_SCALE = D_HEAD ** -0.5

LANES = 128
SUBLANES = 8
VMEM_LIMIT_BYTES = 56 * 1024 * 1024

NEG_BIG = -1e30


def _softplus(z):
    return jnp.maximum(z, 0.0) + jnp.log(1.0 + jnp.exp(-jnp.abs(z)))


def _sigmoid(z):
    return 1.0 / (1.0 + jnp.exp(-z))


def _split_bf16(x):
    hi = x.astype(BF16)
    lo = (x - hi.astype(F32)).astype(BF16)
    return hi, lo


def _params(*sem):
    return pltpu.CompilerParams(dimension_semantics=sem, vmem_limit_bytes=VMEM_LIMIT_BYTES)


def _rope_table_kernel(freq_ref, sgn_ref, cos_ref, sin_ref, *, pos0, period):
    rows = cos_ref.shape[0]
    r = lax.broadcasted_iota(jnp.int32, (rows, LANES), 0)
    pos = (pos0 + r % period).astype(F32)
    ang = pos * freq_ref[...]
    cos_ref[...] = jnp.cos(ang)
    sin_ref[...] = jnp.sin(ang) * sgn_ref[...]


def _rope_tables(rows, pos0, period):
    half = ROT_DIM // 2
    freqs = ROPE_THETA ** (-jnp.arange(half, dtype=F32) / half)
    d = jnp.arange(LANES) % D_HEAD
    freq_lane = jnp.where(d < ROT_DIM, freqs[d % half], 0.0).astype(F32)[None, :]
    sgn_lane = jnp.where(d < half, -1.0, jnp.where(d < ROT_DIM, 1.0, 0.0)).astype(F32)[None, :]
    return pl.pallas_call(
        functools.partial(_rope_table_kernel, pos0=pos0, period=period),
        out_shape=(jax.ShapeDtypeStruct((rows, LANES), F32),) * 2,
        name="rope_table",
    )(freq_lane, sgn_lane)


_COL_Q_SB, _COL_K_SB, _COL_V_SB, _COL_G_SB = 0, 1, 2, 3
_COL_X_RG, _COL_G_RG = 4, 5
_COL_Q_DF, _COL_K_DF, _COL_V_DF, _COL_G_DF = 6, 7, 8, 9
_COL_GATE = 10


def _project(x_ref, w_ref, b_ref):
    xb = x_ref[...].astype(BF16)

    def proj(col, sub=0, width=W_ATT):
        off = col * W_ATT + sub * width
        return (jnp.dot(xb, w_ref[:, off:off + width], preferred_element_type=F32)
                + b_ref[:, off:off + width])

    return proj


def _rope(h, cos, sin):
    d = lax.broadcasted_iota(jnp.int32, cos.shape, 1) % D_HEAD
    first_half = d < (ROT_DIM // 2)
    out = []
    for j in range(W_ATT // LANES):
        xj = h[:, j * LANES:(j + 1) * LANES]
        up = pltpu.roll(xj, LANES - ROT_DIM // 2, 1)
        dn = pltpu.roll(xj, ROT_DIM // 2, 1)
        out.append(xj * cos + jnp.where(first_half, up, dn) * sin)
    return out


def _write_gate(proj, gate):
    d_model = gate.shape[1] // N_BRANCH
    for j in range(N_BRANCH):
        gate[:, j * d_model:(j + 1) * d_model] = proj(_COL_GATE, j, d_model)


def _in_proj_decode_kernel(x_ref, w_ref, b_ref, cos_ref, sin_ref,
                           q_sb, k_sb, v_sb, g_sb, x_rg, g_rg, q_df, k_df, v_df, g_df, gate):
    proj = _project(x_ref, w_ref, b_ref)
    cos, sin = cos_ref[...], sin_ref[...]
    q_sb[...] = proj(_COL_Q_SB) * QK_SCALE
    k_sb[...] = proj(_COL_K_SB)
    v_sb[...] = proj(_COL_V_SB)
    g_sb[...] = proj(_COL_G_SB)
    x_rg[...] = proj(_COL_X_RG)
    g_rg[...] = proj(_COL_G_RG)
    for j, r in enumerate(_rope(proj(_COL_Q_DF), cos, sin)):
        q_df[:, j * LANES:(j + 1) * LANES] = r * QK_SCALE
    for j, r in enumerate(_rope(proj(_COL_K_DF), cos, sin)):
        k_df[:, j * LANES:(j + 1) * LANES] = r
    v_df[...] = proj(_COL_V_DF)
    g_df[...] = proj(_COL_G_DF)
    _write_gate(proj, gate)


_N_PROMPT_OUTPUTS = 15


def _in_proj_prompt_kernel(x_ref, w_ref, b_ref, cos_ref, sin_ref, *rest):
    (q_sb, k_sb, v_sb, kt_sb, vt_sb, g_sb, x_rg, g_rg,
     q_df, k_df, v_df, k4_df, v4_df, g_df, gate) = rest[len(rest) - _N_PROMPT_OUTPUTS:]
    tm = x_ref.shape[0]
    proj = _project(x_ref, w_ref, b_ref)
    cos, sin = cos_ref[...], sin_ref[...]
    q_sb[...] = (proj(_COL_Q_SB) * QK_SCALE).astype(BF16)
    k = proj(_COL_K_SB)
    k_sb[...] = k.astype(BF16)
    kt_sb[...] = k.T
    v = proj(_COL_V_SB)
    v_sb[...] = v.astype(BF16)
    vt_sb[...] = v.T
    g_sb[...] = proj(_COL_G_SB)
    x_rg[...] = proj(_COL_X_RG)
    g_rg[...] = proj(_COL_G_RG)
    for j, r in enumerate(_rope(proj(_COL_Q_DF), cos, sin)):
        q_df[:, j * LANES:(j + 1) * LANES] = (r * QK_SCALE).astype(BF16)
    for j, r in enumerate(_rope(proj(_COL_K_DF), cos, sin)):
        k_df[:, j * LANES:(j + 1) * LANES] = r.astype(BF16)
        k4_df[pl.ds(j, tm, stride=H_DIFF), :] = r
    v = proj(_COL_V_DF)
    v_df[...] = v.astype(BF16)
    for j in range(H_DIFF):
        v4_df[pl.ds(j, tm, stride=H_DIFF), :] = v[:, j * LANES:(j + 1) * LANES]
    g_df[...] = proj(_COL_G_DF)
    _write_gate(proj, gate)


def _in_proj_common_specs(x2d, w_bf, cos_t, tm):
    d_model = x2d.shape[1]
    d_in = w_bf.shape[1]
    n_t = cos_t.shape[0] // tm
    row = lambda i: (i, 0)
    const = lambda i: (0, 0)
    tab = lambda i: (i % n_t, 0)
    return [pl.BlockSpec((tm, d_model), row),
            pl.BlockSpec((d_model, d_in), const, pipeline_mode=pl.Buffered(1)),
            pl.BlockSpec((1, d_in), const, pipeline_mode=pl.Buffered(1)),
            pl.BlockSpec((tm, LANES), tab),
            pl.BlockSpec((tm, LANES), tab)]


def _in_proj_decode(x2d, w_bf, b, cos_t, sin_t):
    m, d_model = x2d.shape
    row = lambda i: (i, 0)
    att = jax.ShapeDtypeStruct((m, W_ATT), F32)
    out_shape = (att,) * 10 + (jax.ShapeDtypeStruct((m, N_BRANCH * d_model), F32),)
    return pl.pallas_call(
        _in_proj_decode_kernel,
        grid=(1,),
        in_specs=_in_proj_common_specs(x2d, w_bf, cos_t, m),
        out_specs=tuple(pl.BlockSpec((m, s.shape[1]), row) for s in out_shape),
        out_shape=out_shape,
        compiler_params=_params("arbitrary"),
        name="in_proj_decode",
    )(x2d, w_bf, b, cos_t, sin_t)


def _in_proj_prompt(x2d, w_bf, b, cos_t, sin_t, stacked, layer, depth, batch, seq, tm):
    m, d_model = x2d.shape
    n_t = seq // tm
    row = lambda i: (i, 0)
    act = lambda dt: jax.ShapeDtypeStruct((m, W_ATT), dt)
    t_shape = jax.ShapeDtypeStruct((depth, batch, W_ATT, seq), F32)
    r_shape = jax.ShapeDtypeStruct((depth, m * H_DIFF, LANES), F32)
    t_spec = pl.BlockSpec((None, None, W_ATT, tm), lambda i: (layer, i // n_t, 0, i % n_t))
    r_spec = pl.BlockSpec((None, tm * H_DIFF, LANES), lambda i: (layer, i, 0))
    a_spec = pl.BlockSpec((tm, W_ATT), row)
    out_shape = (act(BF16), act(BF16), act(BF16), t_shape, t_shape, act(F32), act(F32), act(F32),
                 act(BF16), act(BF16), act(BF16), r_shape, r_shape, act(F32),
                 jax.ShapeDtypeStruct((m, N_BRANCH * d_model), F32))
    out_specs = ((a_spec,) * 3 + (t_spec, t_spec) + (a_spec,) * 6
                 + (r_spec, r_spec, a_spec, pl.BlockSpec((tm, N_BRANCH * d_model), row)))
    assert len(out_shape) == _N_PROMPT_OUTPUTS
    in_specs = _in_proj_common_specs(x2d, w_bf, cos_t, tm)
    args = [x2d, w_bf, b, cos_t, sin_t]
    aliases = {}
    if stacked is not None:
        for arr, out_idx in zip(stacked, (3, 4, 11, 12)):
            aliases[len(args)] = out_idx
            in_specs.append(pl.BlockSpec(memory_space=pl.ANY))
            args.append(arr)
    return pl.pallas_call(
        _in_proj_prompt_kernel,
        grid=(m // tm,),
        in_specs=in_specs,
        out_specs=out_specs,
        out_shape=out_shape,
        input_output_aliases=aliases,
        compiler_params=_params("parallel"),
        name="in_proj_prompt",
    )(*args)


SB_GROUP_SIZES = (1, 2)
DIFF_GROUP_SIZES = (1, 2, 4)

_NT = (((1,), (1,)), ((), ()))
_TN = (((0,), (0,)), ((), ()))


def _stacked_queries(q):
    lane = lax.broadcasted_iota(jnp.int32, q.shape, 1)
    zero = jnp.zeros_like(q)
    return jnp.concatenate([jnp.where(lane < D_HEAD, q, zero), jnp.where(lane >= D_HEAD, q, zero)],
                           axis=0)


def _diag_mask(tk, tq, offset, strict):
    s_idx = lax.broadcasted_iota(jnp.int32, (tk, 2 * tq), 0) + offset
    t_idx = lax.broadcasted_iota(jnp.int32, (tk, 2 * tq), 1) % tq
    return (s_idx < t_idx) if strict else (s_idx <= t_idx)


def _sweep(i, ratio, first, group, group_sizes, state):
    for d in reversed(range(ratio)):
        state = first(i * ratio + d, d, state)
    pos = i * ratio
    for size, nxt in zip(group_sizes, group_sizes[1:] + (None,)):
        count = pos // size if nxt is None else (pos // size) % (nxt // size)
        body = lambda j, s, size=size, pos=pos: group(size, pos - 1 - size * j, s)
        state = lax.fori_loop(0, count, body, state)
        pos = pos - count * size
    return state


def _sb_prompt_kernel(q_ref, k_ref, v_ref, o_ref, acc_ref, *, tq, tk):
    i = pl.program_id(2)
    q2 = _stacked_queries(q_ref[...])
    s_idx = lax.broadcasted_iota(jnp.int32, (tk, 2 * tk), 0)
    j_idx = lax.broadcasted_iota(jnp.int32, (tk, 2 * tk), 1) % tk
    later_keys = (j_idx > s_idx).astype(BF16)

    acc_ref[...] = jnp.zeros_like(acc_ref)

    def chain(kb, carry, diag):
        start = pl.multiple_of(kb * tk, tk)
        k = k_ref[pl.ds(start, tk), :]
        v = v_ref[pl.ds(start, tk), :]
        z = lax.dot_general(k, q2, _NT, preferred_element_type=F32)
        sp = _softplus(z)
        if diag is not None:
            valid = _diag_mask(tk, tq, diag * tk, True)
            sp = jnp.where(valid, sp, 0.0)
        hi, lo = _split_bf16(sp)
        later = jnp.dot(later_keys, jnp.concatenate([hi, lo], axis=0),
                        preferred_element_type=F32)
        w = jnp.exp(z - sp - later + carry)
        if diag is not None:
            w = jnp.where(valid, w, 0.0)
        pv = lax.dot_general(v, w.astype(BF16), _TN, preferred_element_type=F32)
        return pv, carry - (later[0:1, :] + sp[0:1, :])

    def first(kb, d, carry):
        pv, carry = chain(kb, carry, d)
        acc_ref[...] += pv
        return carry

    def group(size, kb, carry):
        total = None
        for g in range(size):
            pv, carry = chain(kb - g, carry, None)
            total = pv if total is None else total + pv
        acc_ref[...] += total
        return carry

    _sweep(i, tq // tk, first, group, SB_GROUP_SIZES, jnp.zeros((1, 2 * tq), F32))

    o_t = jnp.concatenate([acc_ref[0:D_HEAD, 0:tq], acc_ref[D_HEAD:2 * D_HEAD, tq:2 * tq]], axis=0)
    o_ref[...] = o_t.T


def _sb_prompt(q, k, v, batch, seq, tq, tk):
    m = q.shape[0]
    nq = seq // tq
    hp = H_SB // 2
    return pl.pallas_call(
        functools.partial(_sb_prompt_kernel, tq=tq, tk=tk),
        grid=(batch, hp, nq),
        in_specs=[pl.BlockSpec((tq, LANES), lambda b, h, i: (b * nq + i, h)),
                  pl.BlockSpec((seq, LANES), lambda b, h, i: (b, h)),
                  pl.BlockSpec((seq, LANES), lambda b, h, i: (b, h))],
        out_specs=pl.BlockSpec((tq, LANES), lambda b, h, i: (b * nq + i, h)),
        out_shape=jax.ShapeDtypeStruct((m, W_ATT), F32),
        scratch_shapes=[pltpu.VMEM((LANES, 2 * tq), F32)],
        compiler_params=_params("parallel", "parallel", "parallel"),
        name="sb_prompt",
    )(q, k, v)


def _lambda(lam_ref, lam_init):
    p = lam_ref[...]
    s1 = jnp.sum(p[0:1, :] * p[1:2, :], axis=-1, keepdims=True)
    s2 = jnp.sum(p[2:3, :] * p[3:4, :], axis=-1, keepdims=True)
    return jnp.exp(s1) - jnp.exp(s2) + lam_init


def _diff_prompt_kernel(q_ref, k_ref, v_ref, lam_ref, g_ref, o_ref, acc_ref, *, tq, tk, lam_init):
    i = pl.program_id(2)
    q2 = _stacked_queries(q_ref[...])

    acc_ref[...] = jnp.zeros_like(acc_ref)

    def scores(kb, diag):
        start = pl.multiple_of(kb * tk, tk)
        k = k_ref[pl.ds(start, tk), :]
        z = lax.dot_general(k, q2, _NT, preferred_element_type=F32)
        if diag is not None:
            z = jnp.where(_diag_mask(tk, tq, diag * tk, False), z, NEG_BIG)
        return z

    def update(kbs, zs, state):
        m_old, l_old = state
        m_new = m_old
        for z in zs:
            m_new = jnp.maximum(m_new, jnp.max(z, axis=0, keepdims=True))
        alpha = jnp.exp(m_old - m_new)
        l_new = alpha * l_old
        pv = None
        for kb, z in zip(kbs, zs):
            p = jnp.exp(z - m_new)
            l_new = l_new + jnp.sum(p, axis=0, keepdims=True)
            v = v_ref[pl.ds(pl.multiple_of(kb * tk, tk), tk), :]
            term = lax.dot_general(v, p.astype(BF16), _TN, preferred_element_type=F32)
            pv = term if pv is None else pv + term
        acc_ref[...] = alpha * acc_ref[...] + pv
        return m_new, l_new

    def first(kb, d, state):
        return update((kb,), (scores(kb, d),), state)

    def group(size, kb, state):
        kbs = tuple(kb - g for g in range(size))
        return update(kbs, tuple(scores(b, None) for b in kbs), state)

    init = (jnp.full((1, 2 * tq), NEG_BIG, F32), jnp.zeros((1, 2 * tq), F32))
    _, l = _sweep(i, tq // tk, first, group, DIFF_GROUP_SIZES, init)

    lam = _lambda(lam_ref, lam_init)
    o_n = acc_ref[...] / l
    o_t = o_n[:, 0:tq] - lam * o_n[:, tq:2 * tq]
    ms = jnp.mean(o_t * o_t, axis=0, keepdims=True)
    o_t = o_t * lax.rsqrt(ms + LN_EPS) * g_ref[...] * (1.0 - lam_init)
    o_ref[...] = o_t.T


def _diff_prompt(q, k, v, lam_p, g_col, lam_init, batch, seq, tq, tk):
    m = q.shape[0]
    nq = seq // tq
    return pl.pallas_call(
        functools.partial(_diff_prompt_kernel, tq=tq, tk=tk, lam_init=lam_init),
        grid=(batch, H_DIFF, nq),
        in_specs=[pl.BlockSpec((tq, LANES), lambda b, h, i: (b * nq + i, h)),
                  pl.BlockSpec((seq, LANES), lambda b, h, i: (b, h)),
                  pl.BlockSpec((seq, LANES), lambda b, h, i: (b, h)),
                  pl.BlockSpec((4, D_HEAD), lambda b, h, i: (0, 0)),
                  pl.BlockSpec((2 * D_HEAD, 1), lambda b, h, i: (0, 0))],
        out_specs=pl.BlockSpec((tq, LANES), lambda b, h, i: (b * nq + i, h)),
        out_shape=jax.ShapeDtypeStruct((m, W_ATT), F32),
        scratch_shapes=[pltpu.VMEM((LANES, 2 * tq), F32)],
        compiler_params=_params("parallel", "parallel", "parallel"),
        name="diff_prompt",
    )(q, k, v, lam_p, g_col)


def _rg_gates(xc, wa_ref, ba_ref, wx_ref, bx_ref, lam_ref):
    xb = xc.astype(BF16)
    r = _sigmoid(jnp.dot(xb, wa_ref[...], preferred_element_type=F32) + ba_ref[...])
    g = _sigmoid(jnp.dot(xb, wx_ref[...], preferred_element_type=F32) + bx_ref[...])
    log_a = (-RG_C * _softplus(-lam_ref[...])) * r
    a = jnp.exp(log_a)
    t = jnp.tanh(log_a)
    mult = jnp.sqrt(-2.0 * t / (1.0 - t))
    return a, mult * (g * xc)


def _rglru_prompt_kernel(x_ref, buf0_ref, h0_ref, cw_ref, cb_ref, wa_ref, ba_ref, wx_ref, bx_ref,
                         lam_ref, h_ref, hlast_ref, xbuf, hstate, *, tc):
    c = pl.program_id(1)
    pad = SUBLANES

    @pl.when(c == 0)
    def _():
        xbuf[0:pad, :] = jnp.zeros((pad, D_RNN), F32)
        xbuf[pad - (CONV_W - 1):pad, :] = buf0_ref[...]
        hstate[...] = h0_ref[...]

    @pl.when(c != 0)
    def _():
        xbuf[0:pad, :] = xbuf[tc:tc + pad, :]

    xbuf[pad:pad + tc, :] = x_ref[...]
    xc = cb_ref[...]
    for kk in range(CONV_W):
        off = pad - (CONV_W - 1) + kk
        xc = xc + xbuf[off:off + tc, :] * cw_ref[kk:kk + 1, :]

    a, b = _rg_gates(xc, wa_ref, ba_ref, wx_ref, bx_ref, lam_ref)

    row = lax.broadcasted_iota(jnp.int32, a.shape, 0)
    dist = 1
    while dist < tc:
        keep = row >= dist
        a_sh = pltpu.roll(a, dist, 0)
        b_sh = pltpu.roll(b, dist, 0)
        b = jnp.where(keep, a * b_sh + b, b)
        a = jnp.where(keep, a * a_sh, a)
        dist *= 2
    h = b + a * hstate[...]
    h_ref[...] = h
    hstate[...] = h[tc - 1:tc, :]

    @pl.when(c == pl.num_programs(1) - 1)
    def _():
        hlast_ref[...] = h[tc - 1:tc, :]


def _rglru_prompt(x_rg, buf0, h0, rgp, batch, seq, tc):
    m = x_rg.shape[0]
    nc = seq // tc
    const = lambda b, c: (0, 0)
    wspec = lambda shape: pl.BlockSpec(shape, const)
    return pl.pallas_call(
        functools.partial(_rglru_prompt_kernel, tc=tc),
        grid=(batch, nc),
        in_specs=[pl.BlockSpec((tc, D_RNN), lambda b, c: (b * nc + c, 0)),
                  pl.BlockSpec((None, CONV_W - 1, D_RNN), lambda b, c: (b, 0, 0)),
                  pl.BlockSpec((None, 1, D_RNN), lambda b, c: (b, 0, 0)),
                  wspec((CONV_W, D_RNN)), wspec((1, D_RNN)),
                  wspec((D_RNN, D_RNN)), wspec((1, D_RNN)),
                  wspec((D_RNN, D_RNN)), wspec((1, D_RNN)), wspec((1, D_RNN))],
        out_specs=(pl.BlockSpec((tc, D_RNN), lambda b, c: (b * nc + c, 0)),
                   pl.BlockSpec((None, 1, D_RNN), lambda b, c: (b, 0, 0))),
        out_shape=(jax.ShapeDtypeStruct((m, D_RNN), F32),
                   jax.ShapeDtypeStruct((batch, 1, D_RNN), F32)),
        scratch_shapes=[pltpu.VMEM((tc + SUBLANES, D_RNN), F32), pltpu.VMEM((1, D_RNN), F32)],
        compiler_params=_params("parallel", "arbitrary"),
        name="rglru_prompt",
    )(x_rg, buf0, h0, *rgp)


def _rglru_decode_kernel(x_ref, buf_ref, h0_ref, cw_ref, cb_ref, wa_ref, ba_ref, wx_ref, bx_ref,
                         lam_ref, h_ref, hlast_ref):
    steps = x_ref.shape[0]
    rows = [buf_ref[j] for j in range(CONV_W - 1)] + [x_ref[t] for t in range(steps)]
    h = h0_ref[...]
    for t in range(steps):
        xc = cb_ref[...]
        for kk in range(CONV_W):
            xc = xc + rows[t + kk] * cw_ref[kk:kk + 1, :]
        a, b = _rg_gates(xc, wa_ref, ba_ref, wx_ref, bx_ref, lam_ref)
        h = a * h + b
        h_ref[t] = h
    hlast_ref[...] = h


def _rglru_decode(x_tm, buf_tm, h0, rgp):
    steps, nb, _ = x_tm.shape
    return pl.pallas_call(
        _rglru_decode_kernel,
        out_shape=(jax.ShapeDtypeStruct((steps, nb, D_RNN), F32),
                   jax.ShapeDtypeStruct((nb, D_RNN), F32)),
        compiler_params=pltpu.CompilerParams(vmem_limit_bytes=VMEM_LIMIT_BYTES),
        name="rglru_decode",
    )(x_tm, buf_tm, h0, *rgp)


def _merge_kernel(o_sb, g_sb, h_rg, g_rg, o_df, g_df, gate, x_ref,
                  w_sb, w_rg, w_df, w_out, lng, lnb, y_ref, *, alpha):
    def branch(o_ref, g_ref, w_ref):
        g = g_ref[...]
        act = o_ref[...] * (g * _sigmoid(g))
        return jnp.dot(act.astype(BF16), w_ref[...], preferred_element_type=F32)

    d_model = x_ref.shape[1]
    merged = None
    for j, (o_r, g_r, w_r) in enumerate(((o_sb, g_sb, w_sb), (h_rg, g_rg, w_rg), (o_df, g_df, w_df))):
        term = _sigmoid(gate[:, j * d_model:(j + 1) * d_model]) * branch(o_r, g_r, w_r)
        merged = term if merged is None else merged + term
    z = alpha * x_ref[...] + jnp.dot(merged.astype(BF16), w_out[...], preferred_element_type=F32)
    mu = jnp.mean(z, axis=-1, keepdims=True)
    zc = z - mu
    var = jnp.mean(zc * zc, axis=-1, keepdims=True)
    y_ref[...] = zc * lax.rsqrt(var + LN_EPS) * lng[...] + lnb[...]


def _merge(o_sb, g_sb, h_rg, g_rg, o_df, g_df, gate, x2d, mp, alpha, tm):
    m, d_model = x2d.shape
    row = lambda i: (i, 0)
    const = lambda i: (0, 0)
    act = pl.BlockSpec((tm, W_ATT), row)
    return pl.pallas_call(
        functools.partial(_merge_kernel, alpha=alpha),
        grid=(m // tm,),
        in_specs=[act] * 6 + [pl.BlockSpec((tm, N_BRANCH * d_model), row),
                              pl.BlockSpec((tm, d_model), row)]
                 + [pl.BlockSpec((W_ATT, d_model), const)] * 3
                 + [pl.BlockSpec((d_model, d_model), const),
                    pl.BlockSpec((1, d_model), const), pl.BlockSpec((1, d_model), const)],
        out_specs=pl.BlockSpec((tm, d_model), row),
        out_shape=jax.ShapeDtypeStruct((m, d_model), F32),
        compiler_params=_params("parallel"),
        name="merge",
    )(o_sb, g_sb, h_rg, g_rg, o_df, g_df, gate, x2d, *mp)


PAGES_PER_STEP = 16
KEY_BLOCK = 256
N_GROUPS = W_ATT // D_HEAD


def _rows_per_token(x, lo, hi):
    steps = x.shape[0]
    return jnp.concatenate(
        [jnp.broadcast_to(x[t:t + 1, lo:hi], (N_GROUPS, hi - lo)) for t in range(steps)], axis=0)


def _sb_decode_kernel(pt_ref, q_ref, kn_ref, vn_ref, *rest):
    del pt_ref
    k_pages = rest[:PAGES_PER_STEP]
    v_pages = rest[PAGES_PER_STEP:2 * PAGES_PER_STEP]
    o_ref, acc_ref, carry_ref = rest[2 * PAGES_PER_STEP:]
    j = pl.program_id(1)
    steps = q_ref.shape[0]
    rep = _rows_per_token(q_ref[...], 0, W_ATT)
    row = lax.broadcasted_iota(jnp.int32, rep.shape, 0)
    lane = lax.broadcasted_iota(jnp.int32, rep.shape, 1)
    own = lane // D_HEAD == row % N_GROUPS
    qx = jnp.where(own, rep, 0.0)
    rows = qx.shape[0]
    tok = lax.broadcasted_iota(jnp.int32, (rows, 1), 0) // N_GROUPS

    @pl.when(j == 0)
    def _():
        z, sp, valid = [], [], []
        for jn in range(steps):
            zj = jnp.sum(qx * kn_ref[jn:jn + 1, :], axis=-1, keepdims=True)
            vj = tok > jn
            z.append(zj)
            valid.append(vj)
            sp.append(jnp.where(vj, _softplus(zj), 0.0))
        carry = jnp.zeros((rows, 1), F32)
        acc = jnp.zeros((rows, W_ATT), F32)
        for jn in reversed(range(steps)):
            w = jnp.where(valid[jn], jnp.exp(z[jn] - sp[jn] + carry), 0.0)
            acc = acc + w * vn_ref[jn:jn + 1, :]
            carry = carry - sp[jn]
        acc_ref[...] = acc
        carry_ref[...] = carry

    qb = qx.astype(BF16)
    j_idx = lax.broadcasted_iota(jnp.int32, (2 * KEY_BLOCK, KEY_BLOCK), 0) % KEY_BLOCK
    s_idx = lax.broadcasted_iota(jnp.int32, (2 * KEY_BLOCK, KEY_BLOCK), 1)
    later_keys = (j_idx > s_idx).astype(BF16)
    keys = PAGES_PER_STEP * k_pages[0].shape[1]
    n_blocks = keys // KEY_BLOCK
    blk = lambda x, b: x[:, b * KEY_BLOCK:(b + 1) * KEY_BLOCK]
    kt = jnp.concatenate([r[...] for r in k_pages], axis=1).astype(BF16)
    z = jnp.dot(qb, kt, preferred_element_type=F32)
    sp = _softplus(z)
    hi, lo = _split_bf16(sp)
    later = jnp.concatenate(
        [jnp.dot(jnp.concatenate([blk(hi, b), blk(lo, b)], axis=1), later_keys,
                 preferred_element_type=F32) for b in range(n_blocks)], axis=1)
    carry = carry_ref[...]
    carries = [None] * n_blocks
    for b in reversed(range(n_blocks)):
        carries[b] = jnp.broadcast_to(carry, (rows, KEY_BLOCK))
        first = b * KEY_BLOCK
        carry = carry - (later[:, first:first + 1] + sp[:, first:first + 1])
    w = jnp.exp(z - sp - later + jnp.concatenate(carries, axis=1))
    vt = jnp.concatenate([r[...] for r in v_pages], axis=1).astype(BF16)
    acc = acc_ref[...] + lax.dot_general(w.astype(BF16), vt, _NT, preferred_element_type=F32)
    acc_ref[...] = acc
    carry_ref[...] = carry

    @pl.when(j == pl.num_programs(1) - 1)
    def _():
        o_ref[...] = jnp.sum(jnp.where(own, acc, 0.0).reshape(steps, N_GROUPS, W_ATT), axis=1)


def _diff_decode_kernel(pt_ref, q_ref, kn_ref, vn_ref, lam_ref, g_ref, *rest, lam_init):
    del pt_ref
    k_pages = rest[:PAGES_PER_STEP]
    v_pages = rest[PAGES_PER_STEP:2 * PAGES_PER_STEP]
    o_ref, acc_ref, m_ref, l_ref = rest[2 * PAGES_PER_STEP:]
    j = pl.program_id(1)
    steps = q_ref.shape[0]
    rows = steps * N_GROUPS
    row = lax.broadcasted_iota(jnp.int32, (rows, LANES), 0)
    lane = lax.broadcasted_iota(jnp.int32, (rows, LANES), 1)
    head = (row % N_GROUPS) // 2
    comp = row % 2

    def per_head(x):
        out = jnp.zeros((rows, LANES), F32)
        for h in range(H_DIFF):
            out = jnp.where(head == h, _rows_per_token(x, h * LANES, (h + 1) * LANES), out)
        return out

    def per_head_row(x_row):
        out = jnp.zeros((rows, LANES), F32)
        for h in range(H_DIFF):
            out = jnp.where(head == h,
                            jnp.broadcast_to(x_row[:, h * LANES:(h + 1) * LANES], (rows, LANES)), out)
        return out

    qx = jnp.where((lane // D_HEAD) == comp, per_head(q_ref[...]), 0.0)
    tok = lax.broadcasted_iota(jnp.int32, (rows, 1), 0) // N_GROUPS

    @pl.when(j == 0)
    def _():
        z = []
        m = jnp.full((rows, 1), NEG_BIG, F32)
        for jn in range(steps):
            zj = jnp.sum(qx * per_head_row(kn_ref[jn:jn + 1, :]), axis=-1, keepdims=True)
            zj = jnp.where(tok >= jn, zj, NEG_BIG)
            z.append(zj)
            m = jnp.maximum(m, zj)
        l = jnp.zeros((rows, 1), F32)
        acc = jnp.zeros((rows, LANES), F32)
        for jn in range(steps):
            p = jnp.exp(z[jn] - m)
            l = l + p
            acc = acc + p * per_head_row(vn_ref[jn:jn + 1, :])
        acc_ref[...] = acc
        m_ref[...] = m
        l_ref[...] = l

    qb = qx.astype(BF16)
    n_cols = PAGES_PER_STEP * k_pages[0].shape[0]
    col_head = lax.broadcasted_iota(jnp.int32, (rows, n_cols), 1) % H_DIFF
    row_head = (lax.broadcasted_iota(jnp.int32, (rows, n_cols), 0) % N_GROUPS) // 2
    kb = jnp.concatenate([r[...] for r in k_pages], axis=0).astype(BF16)
    z = lax.dot_general(qb, kb, _NT, preferred_element_type=F32)
    z = jnp.where(col_head == row_head, z, NEG_BIG)
    m_old = m_ref[...]
    m_new = jnp.maximum(m_old, jnp.max(z, axis=-1, keepdims=True))
    alpha = jnp.exp(m_old - m_new)
    p = jnp.exp(z - m_new)
    l = alpha * l_ref[...] + jnp.sum(p, axis=-1, keepdims=True)
    vb = jnp.concatenate([r[...] for r in v_pages], axis=0).astype(BF16)
    acc = alpha * acc_ref[...] + jnp.dot(p.astype(BF16), vb, preferred_element_type=F32)
    acc_ref[...] = acc
    m_ref[...] = m_new
    l_ref[...] = l

    @pl.when(j == pl.num_programs(1) - 1)
    def _():
        lam = _lambda(lam_ref, lam_init)
        comp1 = lax.broadcasted_iota(jnp.int32, (rows, 1), 0) % 2
        scaled = acc * (jnp.where(comp1 == 0, 1.0, -lam) / l)
        o = scaled + pltpu.roll(scaled, rows - 1, 0)
        ms = jnp.mean(o * o, axis=-1, keepdims=True)
        o_ref[...] = o * lax.rsqrt(ms + LN_EPS) * g_ref[...] * (1.0 - lam_init)


def _decode_attention(kernel_fn, layer, page_table, q3, kn3, vn3, cache_k, cache_v, extra, extra_specs,
                      out_cols, scratch, name, newest_first):
    nb, steps, _ = q3.shape
    n_pages = page_table.shape[1]
    n_chunks = n_pages // PAGES_PER_STEP
    page_shape = cache_k.shape[2:]

    def page_spec(p):
        def index(b, j, pt):
            chunk = (n_chunks - 1 - j) if newest_first else j
            return (layer, pt[b, chunk * PAGES_PER_STEP + p], 0, 0)
        return pl.BlockSpec((None, None) + page_shape, index)

    tok_spec = pl.BlockSpec((None, steps, W_ATT), lambda b, j, pt: (b, 0, 0))
    out_rows = steps if out_cols == W_ATT else steps * N_GROUPS
    grid_spec = pltpu.PrefetchScalarGridSpec(
        num_scalar_prefetch=1,
        grid=(nb, n_chunks),
        in_specs=[tok_spec] * 3 + extra_specs + [page_spec(p) for p in range(PAGES_PER_STEP)] * 2,
        out_specs=pl.BlockSpec((None, out_rows, out_cols), lambda b, j, pt: (b, 0, 0)),
        scratch_shapes=scratch,
    )
    return pl.pallas_call(
        kernel_fn,
        grid_spec=grid_spec,
        out_shape=jax.ShapeDtypeStruct((nb, out_rows, out_cols), F32),
        compiler_params=_params("parallel", "arbitrary"),
        name=name,
    )(page_table, q3, kn3, vn3, *extra, *([cache_k] * PAGES_PER_STEP), *([cache_v] * PAGES_PER_STEP))


def _block_diag(w):
    n, bi, bj = w.shape
    eye = jnp.eye(n, dtype=w.dtype)
    return (eye[:, None, :, None] * w[:, :, None, :]).reshape(n * bi, n * bj)


def _row_tile(m, want):
    return want if m % want == 0 else m


def kernel(x_prompt, x_sample, cache_sb_k, cache_sb_v, cache_diff_k, cache_diff_v, state_conv,
           state_rglru, page_table, w_in, b_in, conv_w, conv_b, rg_w_a, rg_b_a, rg_w_x, rg_b_x,
           rg_lambda, lam_q1, lam_k1, lam_q2, lam_k2, subln_g, w_br_sb, w_br_rg, w_br_df, w_out,
           ln_g, ln_b):
    depth = w_in.shape[0]
    batch, seq, d_model = x_prompt.shape
    nb, steps, _ = x_sample.shape
    n_pool, page_rows = cache_sb_k.shape[1], cache_sb_k.shape[2]
    past_len = page_table.shape[1] * page_rows
    alpha = (2 * depth) ** 0.25

    assert cache_sb_k.shape[3:] == (H_SB, D_HEAD) and cache_diff_k.shape[3:] == (H_DIFF, 2 * D_HEAD)
    assert page_table.shape[1] % PAGES_PER_STEP == 0 and KEY_BLOCK % page_rows == 0

    tm_p = _row_tile(seq, 256)
    tq = _row_tile(seq, 1024)
    tk = _row_tile(tq, 256)
    tc = _row_tile(seq, 256)
    m_s = nb * steps

    cos_p, sin_p = _rope_tables(seq, 0, seq)
    cos_s, sin_s = _rope_tables(m_s, past_len, steps)

    sb_pages = lambda c: jnp.transpose(c, (0, 1, 3, 4, 2)).reshape(depth, n_pool, W_ATT, page_rows)
    df_pages = lambda c: c.reshape(depth, n_pool, page_rows * H_DIFF, 2 * D_HEAD)
    c_sb_k, c_sb_v = sb_pages(cache_sb_k), sb_pages(cache_sb_v)
    c_df_k, c_df_v = df_pages(cache_diff_k), df_pages(cache_diff_v)

    xp = x_prompt.reshape(batch * seq, d_model)
    xs = x_sample.reshape(m_s, d_model)
    zeros_buf = jnp.zeros((batch, CONV_W - 1, D_RNN), F32)
    zeros_h = jnp.zeros((batch, 1, D_RNN), F32)

    stacked = None
    p_conv, p_h, new_s = [], [], []
    for l in range(depth):
        lam_init = 0.8 - 0.6 * math.exp(-0.3 * l)
        w_bf = w_in[l].astype(BF16)
        b_l = b_in[l][None, :]
        rgp = (conv_w[l], conv_b[l][None, :],
               _block_diag(rg_w_a[l]).astype(BF16), rg_b_a[l][None, :],
               _block_diag(rg_w_x[l]).astype(BF16), rg_b_x[l][None, :], rg_lambda[l][None, :])
        mp = (w_br_sb[l].astype(BF16), w_br_rg[l].astype(BF16), w_br_df[l].astype(BF16),
              w_out[l].astype(BF16), ln_g[l][None, :], ln_b[l][None, :])
        lam_p = jnp.stack([lam_q1[l], lam_k1[l], lam_q2[l], lam_k2[l]])
        g_row = subln_g[l][None, :]
        g_col = subln_g[l][:, None]

        (q_sb, k_sb, v_sb, kt_sb, vt_sb, g_sb, x_rg, g_rg, q_df, k_df, v_df, k4_df, v4_df, g_df,
         gate) = _in_proj_prompt(xp, w_bf, b_l, cos_p, sin_p, stacked, l, depth, batch, seq, tm_p)
        stacked = (kt_sb, vt_sb, k4_df, v4_df)
        o_sb = _sb_prompt(q_sb, k_sb, v_sb, batch, seq, tq, tk)
        o_df = _diff_prompt(q_df, k_df, v_df, lam_p, g_col, lam_init, batch, seq, tq, tk)
        h_rg, h_last = _rglru_prompt(x_rg, zeros_buf, zeros_h, rgp, batch, seq, tc)
        xp = _merge(o_sb, g_sb, h_rg, g_rg, o_df, g_df, gate, xp, mp, alpha, tm_p)
        x_rg3 = x_rg.reshape(batch, seq, D_RNN)
        p_conv.append(jnp.concatenate([zeros_buf, x_rg3], axis=1)[:, seq:, :])
        p_h.append(h_last.reshape(batch, D_RNN))

        (q_sb, k_sb, v_sb, g_sb, x_rg, g_rg, q_df, k_df, v_df, g_df, gate) = _in_proj_decode(
            xs, w_bf, b_l, cos_s, sin_s)
        tok3 = lambda a: a.reshape(nb, steps, W_ATT)
        rows = steps * N_GROUPS
        o_sb = _decode_attention(
            _sb_decode_kernel, l, page_table, tok3(q_sb), tok3(k_sb), tok3(v_sb), c_sb_k, c_sb_v,
            [], [], W_ATT,
            [pltpu.VMEM((rows, W_ATT), F32), pltpu.VMEM((rows, 1), F32)],
            "sb_decode", True)
        o_df = _decode_attention(
            functools.partial(_diff_decode_kernel, lam_init=lam_init), l, page_table,
            tok3(q_df), tok3(k_df), tok3(v_df), c_df_k, c_df_v,
            [lam_p, g_row],
            [pl.BlockSpec((4, D_HEAD), lambda b, j, pt: (0, 0)),
             pl.BlockSpec((1, 2 * D_HEAD), lambda b, j, pt: (0, 0))], LANES,
            [pltpu.VMEM((rows, LANES), F32), pltpu.VMEM((rows, 1), F32), pltpu.VMEM((rows, 1), F32)],
            "diff_decode", False)
        o_df = o_df[:, ::2, :].reshape(m_s, W_ATT)
        x_rg3 = x_rg.reshape(nb, steps, D_RNN)
        h_tm, h_last = _rglru_decode(jnp.swapaxes(x_rg3, 0, 1), jnp.swapaxes(state_conv[l], 0, 1),
                                     state_rglru[l], rgp)
        h_rg = jnp.swapaxes(h_tm, 0, 1).reshape(m_s, D_RNN)
        xs = _merge(o_sb.reshape(m_s, W_ATT), g_sb, h_rg, g_rg, o_df, g_df, gate, xs, mp, alpha, m_s)
        new_buf = jnp.concatenate([state_conv[l], x_rg3], axis=1)[:, steps:, :]
        new_s.append((k_sb.reshape(nb, steps, H_SB, D_HEAD), v_sb.reshape(nb, steps, H_SB, D_HEAD),
                      k_df.reshape(nb, steps, H_DIFF, 2 * D_HEAD),
                      v_df.reshape(nb, steps, H_DIFF, 2 * D_HEAD), new_buf, h_last))

    kt_sb, vt_sb, k4_df, v4_df = stacked
    sb_rows = lambda a: jnp.transpose(a.reshape(depth, batch, H_SB, D_HEAD, seq), (0, 1, 4, 2, 3))
    df_rows = lambda a: a.reshape(depth, batch, seq, H_DIFF, 2 * D_HEAD)
    stack = lambda rows_, idx: jnp.stack([r[idx] for r in rows_])
    return ((xp.reshape(batch, seq, d_model), xs.reshape(nb, steps, d_model),
             sb_rows(kt_sb), sb_rows(vt_sb), df_rows(k4_df), df_rows(v4_df),
             jnp.stack(p_conv), jnp.stack(p_h))
            + tuple(stack(new_s, i) for i in range(6)))
```

```python
import functools
import math

import jax
import jax.numpy as jnp
from jax import lax
from jax.experimental import pallas as pl
from jax.experimental.pallas import tpu as pltpu

F32 = jnp.float32
BF16 = jnp.bfloat16

D_HEAD = 64
H_SB = 8
H_DIFF = 4
W_ATT = 512
D_RNN = 512
RG_C = 8.0
CONV_W = 4
ROT_DIM = D_HEAD // 4
ROPE_THETA = 500000.0
N_BRANCH = 3
LN_EPS = 1e-5
QK_SCALE = D_HEAD ** -0.5

LANES = 128
SUBLANES = 8
VMEM_LIMIT_BYTES = 56 * 1024 * 1024

NEG_BIG = -1e30


def _softplus(z):
    return jnp.maximum(z, 0.0) + jnp.log(1.0 + jnp.exp(-jnp.abs(z)))


def _sigmoid(z):
    return 1.0 / (1.0 + jnp.exp(-z))


def _split_bf16(x):
    hi = x.astype(BF16)
    lo = (x - hi.astype(F32)).astype(BF16)
    return hi, lo


def _params(*sem):
    return pltpu.CompilerParams(dimension_semantics=sem, vmem_limit_bytes=VMEM_LIMIT_BYTES)


def _rope_table_kernel(freq_ref, sgn_ref, cos_ref, sin_ref, *, pos0, period):
    rows = cos_ref.shape[0]
    r = lax.broadcasted_iota(jnp.int32, (rows, LANES), 0)
    pos = (pos0 + r % period).astype(F32)
    ang = pos * freq_ref[...]
    cos_ref[...] = jnp.cos(ang)
    sin_ref[...] = jnp.sin(ang) * sgn_ref[...]


def _rope_tables(rows, pos0, period):
    half = ROT_DIM // 2
    freqs = ROPE_THETA ** (-jnp.arange(half, dtype=F32) / half)
    d = jnp.arange(LANES) % D_HEAD
    freq_lane = jnp.where(d < ROT_DIM, freqs[d % half], 0.0).astype(F32)[None, :]
    sgn_lane = jnp.where(d < half, -1.0, jnp.where(d < ROT_DIM, 1.0, 0.0)).astype(F32)[None, :]
    return pl.pallas_call(
        functools.partial(_rope_table_kernel, pos0=pos0, period=period),
        out_shape=(jax.ShapeDtypeStruct((rows, LANES), F32),) * 2,
        name="rope_table",
    )(freq_lane, sgn_lane)


_COL_Q_SB, _COL_K_SB, _COL_V_SB, _COL_G_SB = 0, 1, 2, 3
_COL_X_RG, _COL_G_RG = 4, 5
_COL_Q_DF, _COL_K_DF, _COL_V_DF, _COL_G_DF = 6, 7, 8, 9
_COL_GATE = 10


def _project(x_ref, w_ref, b_ref):
    xb = x_ref[...].astype(BF16)

    def proj(col, sub=0, width=W_ATT):
        off = col * W_ATT + sub * width
        return (jnp.dot(xb, w_ref[:, off:off + width], preferred_element_type=F32)
                + b_ref[:, off:off + width])

    return proj


def _rope(h, cos, sin):
    d = lax.broadcasted_iota(jnp.int32, cos.shape, 1) % D_HEAD
    first_half = d < (ROT_DIM // 2)
    out = []
    for j in range(W_ATT // LANES):
        xj = h[:, j * LANES:(j + 1) * LANES]
        up = pltpu.roll(xj, LANES - ROT_DIM // 2, 1)
        dn = pltpu.roll(xj, ROT_DIM // 2, 1)
        out.append(xj * cos + jnp.where(first_half, up, dn) * sin)
    return out


def _write_gate(proj, gate):
    d_model = gate.shape[1] // N_BRANCH
    for j in range(N_BRANCH):
        gate[:, j * d_model:(j + 1) * d_model] = proj(_COL_GATE, j, d_model)


def _in_proj_decode_kernel(x_ref, w_ref, b_ref, cos_ref, sin_ref,
                           q_sb, k_sb, v_sb, g_sb, x_rg, g_rg, q_df, k_df, v_df, g_df, gate):
    proj = _project(x_ref, w_ref, b_ref)
    cos, sin = cos_ref[...], sin_ref[...]
    q_sb[...] = proj(_COL_Q_SB) * QK_SCALE
    k_sb[...] = proj(_COL_K_SB)
    v_sb[...] = proj(_COL_V_SB)
    g_sb[...] = proj(_COL_G_SB)
    x_rg[...] = proj(_COL_X_RG)
    g_rg[...] = proj(_COL_G_RG)
    for j, r in enumerate(_rope(proj(_COL_Q_DF), cos, sin)):
        q_df[:, j * LANES:(j + 1) * LANES] = r * QK_SCALE
    for j, r in enumerate(_rope(proj(_COL_K_DF), cos, sin)):
        k_df[:, j * LANES:(j + 1) * LANES] = r
    v_df[...] = proj(_COL_V_DF)
    g_df[...] = proj(_COL_G_DF)
    _write_gate(proj, gate)


_N_PROMPT_OUTPUTS = 15


def _in_proj_prompt_kernel(x_ref, w_ref, b_ref, cos_ref, sin_ref, *rest, layer, creates):
    (q_sb, k_sb, v_sb, kt_sb, vt_sb, g_sb, x_rg, g_rg,
     q_df, k_df, v_df, k4_df, v4_df, g_df, gate) = rest[len(rest) - _N_PROMPT_OUTPUTS:]
    tm = x_ref.shape[0]
    if creates:
        for ref in (kt_sb, vt_sb, k4_df, v4_df):
            for other in range(ref.shape[0]):
                if other != layer:
                    ref[other] = jnp.zeros(ref.shape[1:], F32)
        kt_sb, vt_sb, k4_df, v4_df = (r.at[layer] for r in (kt_sb, vt_sb, k4_df, v4_df))
    proj = _project(x_ref, w_ref, b_ref)
    cos, sin = cos_ref[...], sin_ref[...]
    q_sb[...] = (proj(_COL_Q_SB) * QK_SCALE).astype(BF16)
    k = proj(_COL_K_SB)
    k_sb[...] = k.astype(BF16)
    kt_sb[...] = k.T
    v = proj(_COL_V_SB)
    v_sb[...] = v.astype(BF16)
    vt_sb[...] = v.T
    g_sb[...] = proj(_COL_G_SB)
    x_rg[...] = proj(_COL_X_RG)
    g_rg[...] = proj(_COL_G_RG)
    for j, r in enumerate(_rope(proj(_COL_Q_DF), cos, sin)):
        q_df[:, j * LANES:(j + 1) * LANES] = (r * QK_SCALE).astype(BF16)
    for j, r in enumerate(_rope(proj(_COL_K_DF), cos, sin)):
        k_df[:, j * LANES:(j + 1) * LANES] = r.astype(BF16)
        k4_df[pl.ds(j, tm, stride=H_DIFF), :] = r
    v = proj(_COL_V_DF)
    v_df[...] = v.astype(BF16)
    for j in range(H_DIFF):
        v4_df[pl.ds(j, tm, stride=H_DIFF), :] = v[:, j * LANES:(j + 1) * LANES]
    g_df[...] = proj(_COL_G_DF)
    _write_gate(proj, gate)


def _in_proj_common_specs(x2d, w_bf, cos_t, tm):
    d_model = x2d.shape[1]
    d_in = w_bf.shape[1]
    n_t = cos_t.shape[0] // tm
    row = lambda i: (i, 0)
    const = lambda i: (0, 0)
    tab = lambda i: (i % n_t, 0)
    return [pl.BlockSpec((tm, d_model), row),
            pl.BlockSpec((d_model, d_in), const, pipeline_mode=pl.Buffered(1)),
            pl.BlockSpec((1, d_in), const, pipeline_mode=pl.Buffered(1)),
            pl.BlockSpec((tm, LANES), tab),
            pl.BlockSpec((tm, LANES), tab)]


def _in_proj_decode(x2d, w_bf, b, cos_t, sin_t):
    m, d_model = x2d.shape
    row = lambda i: (i, 0)
    att = jax.ShapeDtypeStruct((m, W_ATT), F32)
    out_shape = (att,) * 10 + (jax.ShapeDtypeStruct((m, N_BRANCH * d_model), F32),)
    return pl.pallas_call(
        _in_proj_decode_kernel,
        grid=(1,),
        in_specs=_in_proj_common_specs(x2d, w_bf, cos_t, m),
        out_specs=tuple(pl.BlockSpec((m, s.shape[1]), row) for s in out_shape),
        out_shape=out_shape,
        compiler_params=_params("arbitrary"),
        name="in_proj_decode",
    )(x2d, w_bf, b, cos_t, sin_t)


def _in_proj_prompt(x2d, w_bf, b, cos_t, sin_t, stacked, layer, depth, batch, seq, tm):
    m, d_model = x2d.shape
    n_t = seq // tm
    row = lambda i: (i, 0)
    act = lambda dt: jax.ShapeDtypeStruct((m, W_ATT), dt)
    t_shape = jax.ShapeDtypeStruct((depth, batch, W_ATT, seq), F32)
    r_shape = jax.ShapeDtypeStruct((depth, m * H_DIFF, LANES), F32)
    creates = stacked is None
    slab, at = (depth, 0) if creates else (None, layer)
    t_spec = pl.BlockSpec((slab, None, W_ATT, tm), lambda i: (at, i // n_t, 0, i % n_t))
    r_spec = pl.BlockSpec((slab, tm * H_DIFF, LANES), lambda i: (at, i, 0))
    a_spec = pl.BlockSpec((tm, W_ATT), row)
    out_shape = (act(BF16), act(BF16), act(BF16), t_shape, t_shape, act(F32), act(F32), act(F32),
                 act(BF16), act(BF16), act(BF16), r_shape, r_shape, act(F32),
                 jax.ShapeDtypeStruct((m, N_BRANCH * d_model), F32))
    out_specs = ((a_spec,) * 3 + (t_spec, t_spec) + (a_spec,) * 6
                 + (r_spec, r_spec, a_spec, pl.BlockSpec((tm, N_BRANCH * d_model), row)))
    assert len(out_shape) == _N_PROMPT_OUTPUTS
    in_specs = _in_proj_common_specs(x2d, w_bf, cos_t, tm)
    args = [x2d, w_bf, b, cos_t, sin_t]
    aliases = {}
    if not creates:
        for arr, out_idx in zip(stacked, (3, 4, 11, 12)):
            aliases[len(args)] = out_idx
            in_specs.append(pl.BlockSpec(memory_space=pl.ANY))
            args.append(arr)
    return pl.pallas_call(
        functools.partial(_in_proj_prompt_kernel, layer=layer, creates=creates),
        grid=(m // tm,),
        in_specs=in_specs,
        out_specs=out_specs,
        out_shape=out_shape,
        input_output_aliases=aliases,
        compiler_params=_params("parallel"),
        name="in_proj_prompt",
    )(*args)


SB_GROUP_SIZES = (1, 2)
DIFF_GROUP_SIZES = (1, 2, 4)

_NT = (((1,), (1,)), ((), ()))
_TN = (((0,), (0,)), ((), ()))


def _stacked_queries(q):
    lane = lax.broadcasted_iota(jnp.int32, q.shape, 1)
    zero = jnp.zeros_like(q)
    return jnp.concatenate([jnp.where(lane < D_HEAD, q, zero), jnp.where(lane >= D_HEAD, q, zero)],
                           axis=0)


def _diag_mask(tk, tq, offset, strict):
    s_idx = lax.broadcasted_iota(jnp.int32, (tk, 2 * tq), 0) + offset
    t_idx = lax.broadcasted_iota(jnp.int32, (tk, 2 * tq), 1) % tq
    return (s_idx < t_idx) if strict else (s_idx <= t_idx)


def _take_cols(x, c0, tq):
    return x if c0 == 0 else jnp.concatenate([x[:, c0:tq], x[:, tq + c0:2 * tq]], axis=1)


def _put_cols(full, part, c0, tq):
    if c0 == 0:
        return part
    w = tq - c0
    return jnp.concatenate([full[:, :c0], part[:, :w], full[:, tq:tq + c0], part[:, w:]], axis=1)


def _update_cols(ref, c0, tq, fn):
    if c0 == 0:
        ref[...] = fn(ref[...])
    else:
        w = tq - c0
        new = fn(jnp.concatenate([ref[:, c0:tq], ref[:, tq + c0:2 * tq]], axis=1))
        ref[:, c0:tq] = new[:, :w]
        ref[:, tq + c0:2 * tq] = new[:, w:]


def _sweep(i, ratio, first, group, group_sizes, state):
    for d in reversed(range(ratio)):
        state = first(i * ratio + d, d, state)
    pos = i * ratio
    for size, nxt in zip(group_sizes, group_sizes[1:] + (None,)):
        count = pos // size if nxt is None else (pos // size) % (nxt // size)
        body = lambda j, s, size=size, pos=pos: group(size, pos - 1 - size * j, s)
        state = lax.fori_loop(0, count, body, state)
        pos = pos - count * size
    return state


def _sb_prompt_kernel(q_ref, k_ref, v_ref, o_ref, acc_ref, *, tq, tk):
    i = pl.program_id(2)
    q2 = _stacked_queries(q_ref[...])
    s_idx = lax.broadcasted_iota(jnp.int32, (tk, 2 * tk), 0)
    j_idx = lax.broadcasted_iota(jnp.int32, (tk, 2 * tk), 1) % tk
    later_keys = (j_idx > s_idx).astype(BF16)

    acc_ref[...] = jnp.zeros_like(acc_ref)

    def chain(kb, carry, diag):
        c0 = 0 if diag is None else diag * tk
        start = pl.multiple_of(kb * tk, tk)
        k = k_ref[pl.ds(start, tk), :]
        v = v_ref[pl.ds(start, tk), :]
        q_use = q2 if c0 == 0 else jnp.concatenate([q2[c0:tq], q2[tq + c0:2 * tq]], axis=0)
        z = lax.dot_general(k, q_use, _NT, preferred_element_type=F32)
        sp = _softplus(z)
        if diag is not None:
            valid = _diag_mask(tk, tq - c0, 0, True)
            sp = jnp.where(valid, sp, 0.0)
        hi, lo = _split_bf16(sp)
        later = jnp.dot(later_keys, jnp.concatenate([hi, lo], axis=0),
                        preferred_element_type=F32)
        w = jnp.exp(z - sp - later + carry)
        if diag is not None:
            w = jnp.where(valid, w, 0.0)
        pv = lax.dot_general(v, w.astype(BF16), _TN, preferred_element_type=F32)
        return pv, carry - (later[0:1, :] + sp[0:1, :])

    def first(kb, d, carry):
        c0 = d * tk
        pv, part = chain(kb, _take_cols(carry, c0, tq), d)
        _update_cols(acc_ref, c0, tq, lambda a: a + pv)
        return _put_cols(carry, part, c0, tq)

    def group(size, kb, carry):
        total = None
        for g in range(size):
            pv, carry = chain(kb - g, carry, None)
            total = pv if total is None else total + pv
        acc_ref[...] += total
        return carry

    _sweep(i, tq // tk, first, group, SB_GROUP_SIZES, jnp.zeros((1, 2 * tq), F32))

    o_t = jnp.concatenate([acc_ref[0:D_HEAD, 0:tq], acc_ref[D_HEAD:2 * D_HEAD, tq:2 * tq]], axis=0)
    o_ref[...] = o_t.T


def _sb_prompt(q, k, v, batch, seq, tq, tk):
    m = q.shape[0]
    nq = seq // tq
    hp = H_SB // 2
    return pl.pallas_call(
        functools.partial(_sb_prompt_kernel, tq=tq, tk=tk),
        grid=(batch, hp, nq),
        in_specs=[pl.BlockSpec((tq, LANES), lambda b, h, i: (b * nq + i, h)),
                  pl.BlockSpec((seq, LANES), lambda b, h, i: (b, h)),
                  pl.BlockSpec((seq, LANES), lambda b, h, i: (b, h))],
        out_specs=pl.BlockSpec((tq, LANES), lambda b, h, i: (b * nq + i, h)),
        out_shape=jax.ShapeDtypeStruct((m, W_ATT), F32),
        scratch_shapes=[pltpu.VMEM((LANES, 2 * tq), F32)],
        compiler_params=_params("parallel", "parallel", "parallel"),
        name="sb_prompt",
    )(q, k, v)


def _lambda(lam_ref, lam_init):
    p = lam_ref[...]
    s1 = jnp.sum(p[0:1, :] * p[1:2, :], axis=-1, keepdims=True)
    s2 = jnp.sum(p[2:3, :] * p[3:4, :], axis=-1, keepdims=True)
    return jnp.exp(s1) - jnp.exp(s2) + lam_init


def _diff_prompt_kernel(q_ref, k_ref, v_ref, lam_ref, g_ref, o_ref, acc_ref, *, tq, tk, lam_init):
    i = pl.program_id(2)
    q2 = _stacked_queries(q_ref[...])

    acc_ref[...] = jnp.zeros_like(acc_ref)

    def scores(kb, diag):
        c0 = 0 if diag is None else diag * tk
        start = pl.multiple_of(kb * tk, tk)
        k = k_ref[pl.ds(start, tk), :]
        q_use = q2 if c0 == 0 else jnp.concatenate([q2[c0:tq], q2[tq + c0:2 * tq]], axis=0)
        z = lax.dot_general(k, q_use, _NT, preferred_element_type=F32)
        if diag is not None:
            z = jnp.where(_diag_mask(tk, tq - c0, 0, False), z, NEG_BIG)
        return z

    def update(kbs, zs, state, c0=0):
        m_old, l_old = (_take_cols(s, c0, tq) for s in state)
        m_new = m_old
        for z in zs:
            m_new = jnp.maximum(m_new, jnp.max(z, axis=0, keepdims=True))
        alpha = jnp.exp(m_old - m_new)
        l_new = alpha * l_old
        pv = None
        for kb, z in zip(kbs, zs):
            p = jnp.exp(z - m_new)
            l_new = l_new + jnp.sum(p, axis=0, keepdims=True)
            v = v_ref[pl.ds(pl.multiple_of(kb * tk, tk), tk), :]
            term = lax.dot_general(v, p.astype(BF16), _TN, preferred_element_type=F32)
            pv = term if pv is None else pv + term
        _update_cols(acc_ref, c0, tq, lambda a: alpha * a + pv)
        return _put_cols(state[0], m_new, c0, tq), _put_cols(state[1], l_new, c0, tq)

    def first(kb, d, state):
        return update((kb,), (scores(kb, d),), state, d * tk)

    def group(size, kb, state):
        kbs = tuple(kb - g for g in range(size))
        return update(kbs, tuple(scores(b, None) for b in kbs), state)

    init = (jnp.full((1, 2 * tq), NEG_BIG, F32), jnp.zeros((1, 2 * tq), F32))
    _, l = _sweep(i, tq // tk, first, group, DIFF_GROUP_SIZES, init)

    lam = _lambda(lam_ref, lam_init)
    o_n = acc_ref[...] / l
    o_t = o_n[:, 0:tq] - lam * o_n[:, tq:2 * tq]
    ms = jnp.mean(o_t * o_t, axis=0, keepdims=True)
    o_t = o_t * lax.rsqrt(ms + LN_EPS) * g_ref[...] * (1.0 - lam_init)
    o_ref[...] = o_t.T


def _diff_prompt(q, k, v, lam_p, g_col, lam_init, batch, seq, tq, tk):
    m = q.shape[0]
    nq = seq // tq
    return pl.pallas_call(
        functools.partial(_diff_prompt_kernel, tq=tq, tk=tk, lam_init=lam_init),
        grid=(batch, H_DIFF, nq),
        in_specs=[pl.BlockSpec((tq, LANES), lambda b, h, i: (b * nq + i, h)),
                  pl.BlockSpec((seq, LANES), lambda b, h, i: (b, h)),
                  pl.BlockSpec((seq, LANES), lambda b, h, i: (b, h)),
                  pl.BlockSpec((4, D_HEAD), lambda b, h, i: (0, 0)),
                  pl.BlockSpec((2 * D_HEAD, 1), lambda b, h, i: (0, 0))],
        out_specs=pl.BlockSpec((tq, LANES), lambda b, h, i: (b * nq + i, h)),
        out_shape=jax.ShapeDtypeStruct((m, W_ATT), F32),
        scratch_shapes=[pltpu.VMEM((LANES, 2 * tq), F32)],
        compiler_params=_params("parallel", "parallel", "parallel"),
        name="diff_prompt",
    )(q, k, v, lam_p, g_col)


def _rg_gates(xc, wa_ref, ba_ref, wx_ref, bx_ref, lam_ref):
    xb = xc.astype(BF16)
    r = _sigmoid(jnp.dot(xb, wa_ref[...], preferred_element_type=F32) + ba_ref[...])
    g = _sigmoid(jnp.dot(xb, wx_ref[...], preferred_element_type=F32) + bx_ref[...])
    log_a = (-RG_C * _softplus(-lam_ref[...])) * r
    a = jnp.exp(log_a)
    t = jnp.tanh(log_a)
    mult = jnp.sqrt(-2.0 * t / (1.0 - t))
    return a, mult * (g * xc)


def _rglru_prompt_kernel(x_ref, buf0_ref, h0_ref, cw_ref, cb_ref, wa_ref, ba_ref, wx_ref, bx_ref,
                         lam_ref, h_ref, hlast_ref, xbuf, hstate, *, tc):
    c = pl.program_id(1)
    pad = SUBLANES

    @pl.when(c == 0)
    def _():
        xbuf[0:pad, :] = jnp.zeros((pad, D_RNN), F32)
        xbuf[pad - (CONV_W - 1):pad, :] = buf0_ref[...]
        hstate[...] = h0_ref[...]

    @pl.when(c != 0)
    def _():
        xbuf[0:pad, :] = xbuf[tc:tc + pad, :]

    xbuf[pad:pad + tc, :] = x_ref[...]
    xc = cb_ref[...]
    for kk in range(CONV_W):
        off = pad - (CONV_W - 1) + kk
        xc = xc + xbuf[off:off + tc, :] * cw_ref[kk:kk + 1, :]

    a, b = _rg_gates(xc, wa_ref, ba_ref, wx_ref, bx_ref, lam_ref)

    row = lax.broadcasted_iota(jnp.int32, a.shape, 0)
    dist = 1
    while dist < tc:
        keep = row >= dist
        a_sh = pltpu.roll(a, dist, 0)
        b_sh = pltpu.roll(b, dist, 0)
        b = jnp.where(keep, a * b_sh + b, b)
        a = jnp.where(keep, a * a_sh, a)
        dist *= 2
    h = b + a * hstate[...]
    h_ref[...] = h
    hstate[...] = h[tc - 1:tc, :]

    @pl.when(c == pl.num_programs(1) - 1)
    def _():
        hlast_ref[...] = h[tc - 1:tc, :]


def _rglru_prompt(x_rg, buf0, h0, rgp, batch, seq, tc):
    m = x_rg.shape[0]
    nc = seq // tc
    const = lambda b, c: (0, 0)
    wspec = lambda shape: pl.BlockSpec(shape, const)
    return pl.pallas_call(
        functools.partial(_rglru_prompt_kernel, tc=tc),
        grid=(batch, nc),
        in_specs=[pl.BlockSpec((tc, D_RNN), lambda b, c: (b * nc + c, 0)),
                  pl.BlockSpec((None, CONV_W - 1, D_RNN), lambda b, c: (b, 0, 0)),
                  pl.BlockSpec((None, 1, D_RNN), lambda b, c: (b, 0, 0)),
                  wspec((CONV_W, D_RNN)), wspec((1, D_RNN)),
                  wspec((D_RNN, D_RNN)), wspec((1, D_RNN)),
                  wspec((D_RNN, D_RNN)), wspec((1, D_RNN)), wspec((1, D_RNN))],
        out_specs=(pl.BlockSpec((tc, D_RNN), lambda b, c: (b * nc + c, 0)),
                   pl.BlockSpec((None, 1, D_RNN), lambda b, c: (b, 0, 0))),
        out_shape=(jax.ShapeDtypeStruct((m, D_RNN), F32),
                   jax.ShapeDtypeStruct((batch, 1, D_RNN), F32)),
        scratch_shapes=[pltpu.VMEM((tc + SUBLANES, D_RNN), F32), pltpu.VMEM((1, D_RNN), F32)],
        compiler_params=_params("parallel", "arbitrary"),
        name="rglru_prompt",
    )(x_rg, buf0, h0, *rgp)


def _rglru_decode_kernel(x_ref, buf_ref, h0_ref, cw_ref, cb_ref, wa_ref, ba_ref, wx_ref, bx_ref,
                         lam_ref, h_ref, hlast_ref):
    steps = x_ref.shape[0]
    rows = [buf_ref[j] for j in range(CONV_W - 1)] + [x_ref[t] for t in range(steps)]
    h = h0_ref[...]
    for t in range(steps):
        xc = cb_ref[...]
        for kk in range(CONV_W):
            xc = xc + rows[t + kk] * cw_ref[kk:kk + 1, :]
        a, b = _rg_gates(xc, wa_ref, ba_ref, wx_ref, bx_ref, lam_ref)
        h = a * h + b
        h_ref[t] = h
    hlast_ref[...] = h


def _rglru_decode(x_tm, buf_tm, h0, rgp):
    steps, nb, _ = x_tm.shape
    return pl.pallas_call(
        _rglru_decode_kernel,
        out_shape=(jax.ShapeDtypeStruct((steps, nb, D_RNN), F32),
                   jax.ShapeDtypeStruct((nb, D_RNN), F32)),
        compiler_params=pltpu.CompilerParams(vmem_limit_bytes=VMEM_LIMIT_BYTES),
        name="rglru_decode",
    )(x_tm, buf_tm, h0, *rgp)


def _merge_kernel(o_sb, g_sb, h_rg, g_rg, o_df, g_df, gate, x_ref,
                  w_sb, w_rg, w_df, w_out, lng, lnb, y_ref, *, alpha):
    def branch(o_ref, g_ref, w_ref):
        g = g_ref[...]
        act = o_ref[...] * (g * _sigmoid(g))
        return jnp.dot(act.astype(BF16), w_ref[...], preferred_element_type=F32)

    d_model = x_ref.shape[1]
    merged = None
    for j, (o_r, g_r, w_r) in enumerate(((o_sb, g_sb, w_sb), (h_rg, g_rg, w_rg), (o_df, g_df, w_df))):
        term = _sigmoid(gate[:, j * d_model:(j + 1) * d_model]) * branch(o_r, g_r, w_r)
        merged = term if merged is None else merged + term
    z = alpha * x_ref[...] + jnp.dot(merged.astype(BF16), w_out[...], preferred_element_type=F32)
    mu = jnp.mean(z, axis=-1, keepdims=True)
    zc = z - mu
    var = jnp.mean(zc * zc, axis=-1, keepdims=True)
    y_ref[...] = zc * lax.rsqrt(var + LN_EPS) * lng[...] + lnb[...]


def _merge(o_sb, g_sb, h_rg, g_rg, o_df, g_df, gate, x2d, mp, alpha, tm):
    m, d_model = x2d.shape
    row = lambda i: (i, 0)
    const = lambda i: (0, 0)
    act = pl.BlockSpec((tm, W_ATT), row)
    return pl.pallas_call(
        functools.partial(_merge_kernel, alpha=alpha),
        grid=(m // tm,),
        in_specs=[act] * 6 + [pl.BlockSpec((tm, N_BRANCH * d_model), row),
                              pl.BlockSpec((tm, d_model), row)]
                 + [pl.BlockSpec((W_ATT, d_model), const)] * 3
                 + [pl.BlockSpec((d_model, d_model), const),
                    pl.BlockSpec((1, d_model), const), pl.BlockSpec((1, d_model), const)],
        out_specs=pl.BlockSpec((tm, d_model), row),
        out_shape=jax.ShapeDtypeStruct((m, d_model), F32),
        compiler_params=_params("parallel"),
        name="merge",
    )(o_sb, g_sb, h_rg, g_rg, o_df, g_df, gate, x2d, *mp)


PAGES_PER_STEP = 16
KEY_BLOCK = 256
N_GROUPS = W_ATT // D_HEAD


def _rows_per_token(x, lo, hi):
    steps = x.shape[0]
    return jnp.concatenate(
        [jnp.broadcast_to(x[t:t + 1, lo:hi], (N_GROUPS, hi - lo)) for t in range(steps)], axis=0)


def _sb_decode_kernel(pt_ref, q_ref, kn_ref, vn_ref, *rest):
    del pt_ref
    k_pages = rest[:PAGES_PER_STEP]
    v_pages = rest[PAGES_PER_STEP:2 * PAGES_PER_STEP]
    o_ref, acc_ref, carry_ref = rest[2 * PAGES_PER_STEP:]
    j = pl.program_id(1)
    steps = q_ref.shape[0]
    rep = _rows_per_token(q_ref[...], 0, W_ATT)
    row = lax.broadcasted_iota(jnp.int32, rep.shape, 0)
    lane = lax.broadcasted_iota(jnp.int32, rep.shape, 1)
    own = lane // D_HEAD == row % N_GROUPS
    qx = jnp.where(own, rep, 0.0)
    rows = qx.shape[0]
    tok = lax.broadcasted_iota(jnp.int32, (rows, 1), 0) // N_GROUPS

    @pl.when(j == 0)
    def _():
        z, sp, valid = [], [], []
        for jn in range(steps):
            zj = jnp.sum(qx * kn_ref[jn:jn + 1, :], axis=-1, keepdims=True)
            vj = tok > jn
            z.append(zj)
            valid.append(vj)
            sp.append(jnp.where(vj, _softplus(zj), 0.0))
        carry = jnp.zeros((rows, 1), F32)
        acc = jnp.zeros((rows, W_ATT), F32)
        for jn in reversed(range(steps)):
            w = jnp.where(valid[jn], jnp.exp(z[jn] - sp[jn] + carry), 0.0)
            acc = acc + w * vn_ref[jn:jn + 1, :]
            carry = carry - sp[jn]
        acc_ref[...] = acc
        carry_ref[...] = carry

    qb = qx.astype(BF16)
    j_idx = lax.broadcasted_iota(jnp.int32, (2 * KEY_BLOCK, KEY_BLOCK), 0) % KEY_BLOCK
    s_idx = lax.broadcasted_iota(jnp.int32, (2 * KEY_BLOCK, KEY_BLOCK), 1)
    later_keys = (j_idx > s_idx).astype(BF16)
    keys = PAGES_PER_STEP * k_pages[0].shape[1]
    n_blocks = keys // KEY_BLOCK
    blk = lambda x, b: x[:, b * KEY_BLOCK:(b + 1) * KEY_BLOCK]
    kt = jnp.concatenate([r[...] for r in k_pages], axis=1).astype(BF16)
    z = jnp.dot(qb, kt, preferred_element_type=F32)
    sp = _softplus(z)
    hi, lo = _split_bf16(sp)
    later = jnp.concatenate(
        [jnp.dot(jnp.concatenate([blk(hi, b), blk(lo, b)], axis=1), later_keys,
                 preferred_element_type=F32) for b in range(n_blocks)], axis=1)
    carry = carry_ref[...]
    carries = [None] * n_blocks
    for b in reversed(range(n_blocks)):
        carries[b] = jnp.broadcast_to(carry, (rows, KEY_BLOCK))
        first = b * KEY_BLOCK
        carry = carry - (later[:, first:first + 1] + sp[:, first:first + 1])
    w = jnp.exp(z - sp - later + jnp.concatenate(carries, axis=1))
    vt = jnp.concatenate([r[...] for r in v_pages], axis=1).astype(BF16)
    acc = acc_ref[...] + lax.dot_general(w.astype(BF16), vt, _NT, preferred_element_type=F32)
    acc_ref[...] = acc
    carry_ref[...] = carry

    @pl.when(j == pl.num_programs(1) - 1)
    def _():
        o_ref[...] = jnp.sum(jnp.where(own, acc, 0.0).reshape(steps, N_GROUPS, W_ATT), axis=1)


def _diff_decode_kernel(pt_ref, q_ref, kn_ref, vn_ref, lam_ref, g_ref, *rest, lam_init):
    del pt_ref
    k_pages = rest[:PAGES_PER_STEP]
    v_pages = rest[PAGES_PER_STEP:2 * PAGES_PER_STEP]
    o_ref, acc_ref, m_ref, l_ref = rest[2 * PAGES_PER_STEP:]
    j = pl.program_id(1)
    steps = q_ref.shape[0]
    rows = steps * N_GROUPS
    row = lax.broadcasted_iota(jnp.int32, (rows, LANES), 0)
    lane = lax.broadcasted_iota(jnp.int32, (rows, LANES), 1)
    head = (row % N_GROUPS) // 2
    comp = row % 2

    def per_head(x):
        out = jnp.zeros((rows, LANES), F32)
        for h in range(H_DIFF):
            out = jnp.where(head == h, _rows_per_token(x, h * LANES, (h + 1) * LANES), out)
        return out

    def per_head_row(x_row):
        out = jnp.zeros((rows, LANES), F32)
        for h in range(H_DIFF):
            out = jnp.where(head == h,
                            jnp.broadcast_to(x_row[:, h * LANES:(h + 1) * LANES], (rows, LANES)), out)
        return out

    qx = jnp.where((lane // D_HEAD) == comp, per_head(q_ref[...]), 0.0)
    tok = lax.broadcasted_iota(jnp.int32, (rows, 1), 0) // N_GROUPS

    @pl.when(j == 0)
    def _():
        z = []
        m = jnp.full((rows, 1), NEG_BIG, F32)
        for jn in range(steps):
            zj = jnp.sum(qx * per_head_row(kn_ref[jn:jn + 1, :]), axis=-1, keepdims=True)
            zj = jnp.where(tok >= jn, zj, NEG_BIG)
            z.append(zj)
            m = jnp.maximum(m, zj)
        l = jnp.zeros((rows, 1), F32)
        acc = jnp.zeros((rows, LANES), F32)
        for jn in range(steps):
            p = jnp.exp(z[jn] - m)
            l = l + p
            acc = acc + p * per_head_row(vn_ref[jn:jn + 1, :])
        acc_ref[...] = acc
        m_ref[...] = m
        l_ref[...] = l

    qb = qx.astype(BF16)
    n_cols = PAGES_PER_STEP * k_pages[0].shape[0]
    col_head = lax.broadcasted_iota(jnp.int32, (rows, n_cols), 1) % H_DIFF
    row_head = (lax.broadcasted_iota(jnp.int32, (rows, n_cols), 0) % N_GROUPS) // 2
    kb = jnp.concatenate([r[...] for r in k_pages], axis=0).astype(BF16)
    z = lax.dot_general(qb, kb, _NT, preferred_element_type=F32)
    z = jnp.where(col_head == row_head, z, NEG_BIG)
    m_old = m_ref[...]
    m_new = jnp.maximum(m_old, jnp.max(z, axis=-1, keepdims=True))
    alpha = jnp.exp(m_old - m_new)
    p = jnp.exp(z - m_new)
    l = alpha * l_ref[...] + jnp.sum(p, axis=-1, keepdims=True)
    vb = jnp.concatenate([r[...] for r in v_pages], axis=0).astype(BF16)
    acc = alpha * acc_ref[...] + jnp.dot(p.astype(BF16), vb, preferred_element_type=F32)
    acc_ref[...] = acc
    m_ref[...] = m_new
    l_ref[...] = l

    @pl.when(j == pl.num_programs(1) - 1)
    def _():
        lam = _lambda(lam_ref, lam_init)
        comp1 = lax.broadcasted_iota(jnp.int32, (rows, 1), 0) % 2
        scaled = acc * (jnp.where(comp1 == 0, 1.0, -lam) / l)
        o = scaled + pltpu.roll(scaled, rows - 1, 0)
        ms = jnp.mean(o * o, axis=-1, keepdims=True)
        o_ref[...] = o * lax.rsqrt(ms + LN_EPS) * g_ref[...] * (1.0 - lam_init)


def _decode_attention(kernel_fn, layer, page_table, q3, kn3, vn3, cache_k, cache_v, extra, extra_specs,
                      out_cols, scratch, name, newest_first):
    nb, steps, _ = q3.shape
    n_pages = page_table.shape[1]
    n_chunks = n_pages // PAGES_PER_STEP
    page_shape = cache_k.shape[2:]

    def page_spec(p):
        def index(b, j, pt):
            chunk = (n_chunks - 1 - j) if newest_first else j
            return (layer, pt[b, chunk * PAGES_PER_STEP + p], 0, 0)
        return pl.BlockSpec((None, None) + page_shape, index)

    tok_spec = pl.BlockSpec((None, steps, W_ATT), lambda b, j, pt: (b, 0, 0))
    out_rows = steps if out_cols == W_ATT else steps * N_GROUPS
    grid_spec = pltpu.PrefetchScalarGridSpec(
        num_scalar_prefetch=1,
        grid=(nb, n_chunks),
        in_specs=[tok_spec] * 3 + extra_specs + [page_spec(p) for p in range(PAGES_PER_STEP)] * 2,
        out_specs=pl.BlockSpec((None, out_rows, out_cols), lambda b, j, pt: (b, 0, 0)),
        scratch_shapes=scratch,
    )
    return pl.pallas_call(
        kernel_fn,
        grid_spec=grid_spec,
        out_shape=jax.ShapeDtypeStruct((nb, out_rows, out_cols), F32),
        compiler_params=_params("parallel", "arbitrary"),
        name=name,
    )(page_table, q3, kn3, vn3, *extra, *([cache_k] * PAGES_PER_STEP), *([cache_v] * PAGES_PER_STEP))


def _block_diag(w):
    n, bi, bj = w.shape
    eye = jnp.eye(n, dtype=w.dtype)
    return (eye[:, None, :, None] * w[:, :, None, :]).reshape(n * bi, n * bj)


def _row_tile(m, want):
    return want if m % want == 0 else m


def kernel(x_prompt, x_sample, cache_sb_k, cache_sb_v, cache_diff_k, cache_diff_v, state_conv,
           state_rglru, page_table, w_in, b_in, conv_w, conv_b, rg_w_a, rg_b_a, rg_w_x, rg_b_x,
           rg_lambda, lam_q1, lam_k1, lam_q2, lam_k2, subln_g, w_br_sb, w_br_rg, w_br_df, w_out,
           ln_g, ln_b):
    depth = w_in.shape[0]
    batch, seq, d_model = x_prompt.shape
    nb, steps, _ = x_sample.shape
    n_pool, page_rows = cache_sb_k.shape[1], cache_sb_k.shape[2]
    past_len = page_table.shape[1] * page_rows
    alpha = (2 * depth) ** 0.25

    assert cache_sb_k.shape[3:] == (H_SB, D_HEAD) and cache_diff_k.shape[3:] == (H_DIFF, 2 * D_HEAD)
    assert page_table.shape[1] % PAGES_PER_STEP == 0 and KEY_BLOCK % page_rows == 0

    tm_p = _row_tile(seq, 256)
    tq = _row_tile(seq, 1024)
    tk = _row_tile(tq, 256)
    tc = _row_tile(seq, 256)
    m_s = nb * steps

    cos_p, sin_p = _rope_tables(seq, 0, seq)
    cos_s, sin_s = _rope_tables(m_s, past_len, steps)

    sb_pages = lambda c: jnp.transpose(c, (0, 1, 3, 4, 2)).reshape(depth, n_pool, W_ATT, page_rows)
    df_pages = lambda c: c.reshape(depth, n_pool, page_rows * H_DIFF, 2 * D_HEAD)
    c_sb_k, c_sb_v = sb_pages(cache_sb_k), sb_pages(cache_sb_v)
    c_df_k, c_df_v = df_pages(cache_diff_k), df_pages(cache_diff_v)

    xp = x_prompt.reshape(batch * seq, d_model)
    xs = x_sample.reshape(m_s, d_model)
    zeros_buf = jnp.zeros((batch, CONV_W - 1, D_RNN), F32)
    zeros_h = jnp.zeros((batch, 1, D_RNN), F32)

    stacked = None
    p_conv, p_h, new_s = [], [], []
    for l in range(depth):
        lam_init = 0.8 - 0.6 * math.exp(-0.3 * l)
        w_bf = w_in[l].astype(BF16)
        b_l = b_in[l][None, :]
        rgp = (conv_w[l], conv_b[l][None, :],
               _block_diag(rg_w_a[l]).astype(BF16), rg_b_a[l][None, :],
               _block_diag(rg_w_x[l]).astype(BF16), rg_b_x[l][None, :], rg_lambda[l][None, :])
        mp = (w_br_sb[l].astype(BF16), w_br_rg[l].astype(BF16), w_br_df[l].astype(BF16),
              w_out[l].astype(BF16), ln_g[l][None, :], ln_b[l][None, :])
        lam_p = jnp.stack([lam_q1[l], lam_k1[l], lam_q2[l], lam_k2[l]])
        g_row = subln_g[l][None, :]
        g_col = subln_g[l][:, None]

        (q_sb, k_sb, v_sb, kt_sb, vt_sb, g_sb, x_rg, g_rg, q_df, k_df, v_df, k4_df, v4_df, g_df,
         gate) = _in_proj_prompt(xp, w_bf, b_l, cos_p, sin_p, stacked, l, depth, batch, seq, tm_p)
        stacked = (kt_sb, vt_sb, k4_df, v4_df)
        o_sb = _sb_prompt(q_sb, k_sb, v_sb, batch, seq, tq, tk)
        o_df = _diff_prompt(q_df, k_df, v_df, lam_p, g_col, lam_init, batch, seq, tq, tk)
        h_rg, h_last = _rglru_prompt(x_rg, zeros_buf, zeros_h, rgp, batch, seq, tc)
        xp = _merge(o_sb, g_sb, h_rg, g_rg, o_df, g_df, gate, xp, mp, alpha, tm_p)
        x_rg3 = x_rg.reshape(batch, seq, D_RNN)
        p_conv.append(jnp.concatenate([zeros_buf, x_rg3], axis=1)[:, seq:, :])
        p_h.append(h_last.reshape(batch, D_RNN))

        (q_sb, k_sb, v_sb, g_sb, x_rg, g_rg, q_df, k_df, v_df, g_df, gate) = _in_proj_decode(
            xs, w_bf, b_l, cos_s, sin_s)
        tok3 = lambda a: a.reshape(nb, steps, W_ATT)
        rows = steps * N_GROUPS
        o_sb = _decode_attention(
            _sb_decode_kernel, l, page_table, tok3(q_sb), tok3(k_sb), tok3(v_sb), c_sb_k, c_sb_v,
            [], [], W_ATT,
            [pltpu.VMEM((rows, W_ATT), F32), pltpu.VMEM((rows, 1), F32)],
            "sb_decode", True)
        o_df = _decode_attention(
            functools.partial(_diff_decode_kernel, lam_init=lam_init), l, page_table,
            tok3(q_df), tok3(k_df), tok3(v_df), c_df_k, c_df_v,
            [lam_p, g_row],
            [pl.BlockSpec((4, D_HEAD), lambda b, j, pt: (0, 0)),
             pl.BlockSpec((1, 2 * D_HEAD), lambda b, j, pt: (0, 0))], LANES,
            [pltpu.VMEM((rows, LANES), F32), pltpu.VMEM((rows, 1), F32), pltpu.VMEM((rows, 1), F32)],
            "diff_decode", False)
        o_df = o_df[:, ::2, :].reshape(m_s, W_ATT)
        x_rg3 = x_rg.reshape(nb, steps, D_RNN)
        h_tm, h_last = _rglru_decode(jnp.swapaxes(x_rg3, 0, 1), jnp.swapaxes(state_conv[l], 0, 1),
                                     state_rglru[l], rgp)
        h_rg = jnp.swapaxes(h_tm, 0, 1).reshape(m_s, D_RNN)
        xs = _merge(o_sb.reshape(m_s, W_ATT), g_sb, h_rg, g_rg, o_df, g_df, gate, xs, mp, alpha, m_s)
        new_buf = jnp.concatenate([state_conv[l], x_rg3], axis=1)[:, steps:, :]
        new_s.append((k_sb.reshape(nb, steps, H_SB, D_HEAD), v_sb.reshape(nb, steps, H_SB, D_HEAD),
                      k_df.reshape(nb, steps, H_DIFF, 2 * D_HEAD),
                      v_df.reshape(nb, steps, H_DIFF, 2 * D_HEAD), new_buf, h_last))

    kt_sb, vt_sb, k4_df, v4_df = stacked
    sb_rows = lambda a: jnp.transpose(a.reshape(depth, batch, H_SB, D_HEAD, seq), (0, 1, 4, 2, 3))
    df_rows = lambda a: a.reshape(depth, batch, seq, H_DIFF, 2 * D_HEAD)
    stack = lambda rows_, idx: jnp.stack([r[idx] for r in rows_])
    return ((xp.reshape(batch, seq, d_model), xs.reshape(nb, steps, d_model),
             sb_rows(kt_sb), sb_rows(vt_sb), df_rows(k4_df), df_rows(v4_df),
             jnp.stack(p_conv), jnp.stack(p_h))
            + tuple(stack(new_s, i) for i in range(6)))
```
